```python
import jax, jax.numpy as jnp
from jax import lax
import numpy as np

D_MODEL = 2048
BATCH = 2
SEQ = 4096
DEPTH = 2

HEAD_DIM = 128
N_HEADS = D_MODEL // HEAD_DIM
HEADS_DILATED = (3 * N_HEADS) // 8
HEADS_MOBA = (N_HEADS - HEADS_DILATED) // 2
HEADS_STICK = N_HEADS - HEADS_DILATED - HEADS_MOBA
DILATED_PATTERNS = ((128, 1), (512, 4), (2048, 16))
MOBA_BLOCK = 256
MOBA_TOPK = 3
MOBA_Q_CHUNK = 32
SB_BLOCK = 128
D_FF = 5632
N_EXPERTS = 8
TOP_K = 2
EPS = 1e-6

kernel_name = "hybrid_dilated_moba_stickbreaking_moe"


def rms_norm(x, gain):
    xf = x.astype(jnp.float32)
    y = xf * lax.rsqrt(jnp.mean(xf * xf, axis=-1, keepdims=True) + EPS)
    return (y * gain.astype(jnp.float32)).astype(x.dtype)


def alibi_slopes():
    n = HEADS_DILATED + HEADS_MOBA
    s = jnp.asarray(2.0 ** (-8.0 * np.arange(1, n + 1) / n), dtype=jnp.float32)
    return s[:HEADS_DILATED], s[HEADS_DILATED:]


def dilated_branch(q, k, v, slopes, window, dilation):
    B, H, S, Dh = q.shape
    span = window // dilation
    L = S // dilation
    nblk = -(-L // span)
    Lp = nblk * span

    def to_blocks(a):
        a = a.reshape(B, H, L, dilation, Dh).transpose(0, 1, 3, 2, 4)
        a = jnp.pad(a, ((0, 0), (0, 0), (0, 0), (0, Lp - L), (0, 0)))
        return a.reshape(B, H, dilation, nblk, span, Dh)

    def with_prev(a):
        prev = jnp.pad(a, ((0, 0), (0, 0), (0, 0), (1, 0), (0, 0), (0, 0)))[:, :, :, :-1]
        return jnp.concatenate([prev, a], axis=4)

    qb = to_blocks(q)
    kc = with_prev(to_blocks(k))
    vc = with_prev(to_blocks(v))
    s = jnp.einsum('bhrnqd,bhrnkd->bhrnqk', qb, kc) * (Dh ** -0.5)
    qi = jnp.arange(span)[:, None]
    kj = jnp.arange(2 * span)[None, :]
    steps = qi + span - kj
    key_sub = jnp.arange(nblk)[:, None, None] * span + kj[None] - span
    valid = (steps >= 0) & (steps <= span) & (key_sub >= 0)
    dist = (steps * dilation).astype(jnp.float32)
    s = s - slopes[None, :, None, None, None, None] * dist
    s = jnp.where(valid, s, -jnp.inf)
    m = jnp.max(s, axis=-1, keepdims=True)
    p = jnp.exp(s - m)
    l = jnp.sum(p, axis=-1)
    o = jnp.einsum('bhrnqk,bhrnkd->bhrnqd', p, vc) / l[..., None]
    lse = m[..., 0] + jnp.log(l)
    o = o.reshape(B, H, dilation, Lp, Dh)[:, :, :, :L].transpose(0, 1, 3, 2, 4).reshape(B, H, S, Dh)
    lse = lse.reshape(B, H, dilation, Lp)[..., :L].transpose(0, 1, 3, 2).reshape(B, H, S)
    return o, lse


def dilated_attention(q, k, v, slopes):
    results = [dilated_branch(q, k, v, slopes, w, d) for w, d in DILATED_PATTERNS]
    outs = jnp.stack([r[0] for r in results])
    lses = jnp.stack([r[1] for r in results])
    wts = jax.nn.softmax(lses, axis=0)
    return jnp.sum(wts[..., None] * outs, axis=0)


def moba_attention(q, k, v, slopes):
    B, H, S, Dh = q.shape
    nb = -(-S // MOBA_BLOCK)
    Sp = nb * MOBA_BLOCK
    pad = ((0, 0), (0, 0), (0, Sp - S), (0, 0))
    qp, kp, vp = jnp.pad(q, pad), jnp.pad(k, pad), jnp.pad(v, pad)
    kblk = kp.reshape(B, H, nb, MOBA_BLOCK, Dh)
    vblk = vp.reshape(B, H, nb, MOBA_BLOCK, Dh)
    kmean = jnp.mean(kblk, axis=3)
    topk = min(MOBA_TOPK, nb)
    n_chunks = Sp // MOBA_Q_CHUNK
    qc = qp.reshape(B, H, n_chunks, MOBA_Q_CHUNK, Dh).transpose(2, 0, 1, 3, 4)
    scale = Dh ** -0.5
    offs = jnp.arange(MOBA_BLOCK)
    blk_ids = jnp.arange(nb)
    gather = jax.vmap(jax.vmap(lambda blocks, idx: blocks[idx]))

    def step(args):
        qi, c = args
        pos = c * MOBA_Q_CHUNK + jnp.arange(MOBA_Q_CHUNK)
        own = (c * MOBA_Q_CHUNK) // MOBA_BLOCK
        gate = jnp.einsum('bhqd,bhnd->bhqn', qi, kmean)
        past = blk_ids[None, :] < (pos // MOBA_BLOCK)[:, None]
        gate = jnp.where(past, gate, -jnp.inf)
        gval, gidx = lax.top_k(gate, topk)
        ksel = gather(kblk, gidx)
        vsel = gather(vblk, gidx)
        s_sel = jnp.einsum('bhqd,bhqtkd->bhqtk', qi, ksel) * scale
        kpos_sel = gidx[..., None] * MOBA_BLOCK + offs
        dist_sel = (pos[:, None, None] - kpos_sel).astype(jnp.float32)
        s_sel = s_sel - slopes[None, :, None, None, None] * dist_sel
        s_sel = jnp.where((gval > -jnp.inf)[..., None], s_sel, -jnp.inf)
        kown = lax.dynamic_slice_in_dim(kp, own * MOBA_BLOCK, MOBA_BLOCK, axis=2)
        vown = lax.dynamic_slice_in_dim(vp, own * MOBA_BLOCK, MOBA_BLOCK, axis=2)
        s_own = jnp.einsum('bhqd,bhkd->bhqk', qi, kown) * scale
        dist_own = pos[:, None] - (own * MOBA_BLOCK + offs)[None, :]
        s_own = s_own - slopes[None, :, None, None] * dist_own.astype(jnp.float32)
        s_own = jnp.where(dist_own >= 0, s_own, -jnp.inf)
        s = jnp.concatenate([s_sel.reshape(B, H, MOBA_Q_CHUNK, topk * MOBA_BLOCK), s_own], axis=-1)
        p = jax.nn.softmax(s, axis=-1)
        p_sel = p[..., :topk * MOBA_BLOCK].reshape(B, H, MOBA_Q_CHUNK, topk, MOBA_BLOCK)
        p_own = p[..., topk * MOBA_BLOCK:]
        return (jnp.einsum('bhqtk,bhqtkd->bhqd', p_sel, vsel)
                + jnp.einsum('bhqk,bhkd->bhqd', p_own, vown))

    out = lax.map(step, (qc, jnp.arange(n_chunks)))
    return out.transpose(1, 2, 0, 3, 4).reshape(B, H, Sp, Dh)[:, :, :S]


def stick_breaking_attention(q, k, v):
    B, H, S, Dh = q.shape
    nq = S // SB_BLOCK
    qc = q.reshape(B, H, nq, SB_BLOCK, Dh).transpose(2, 0, 1, 3, 4)
    kpos = jnp.arange(S)
    scale = Dh ** -0.5

    def step(args):
        qi, c = args
        pos = c * SB_BLOCK + jnp.arange(SB_BLOCK)
        z = jnp.einsum('bhqd,bhkd->bhqk', qi, k) * scale
        past = kpos[None, :] < pos[:, None]
        log_1m = jnp.where(past, jax.nn.log_sigmoid(-z), 0.0)
        after = lax.cumsum(log_1m, axis=3, reverse=True) - log_1m
        a = jnp.where(past, jnp.exp(jax.nn.log_sigmoid(z) + after), 0.0)
        return jnp.einsum('bhqk,bhkd->bhqd', a, v)

    out = lax.map(step, (qc, jnp.arange(nq)))
    return out.transpose(1, 2, 0, 3, 4).reshape(B, H, S, Dh)


def hybrid_mixer(h, w_in, q_gain, k_gain, out_gain, w_out, slopes_a, slopes_b):
    B, S, D = h.shape
    qkv = (h @ w_in).reshape(B, S, 3, N_HEADS, HEAD_DIM).transpose(2, 0, 3, 1, 4)
    q, k, v = qkv[0], qkv[1], qkv[2]
    a1 = HEADS_DILATED
    a2 = HEADS_DILATED + HEADS_MOBA
    f32 = jnp.float32
    qa = rms_norm(q[:, :a1], q_gain[0]).astype(f32)
    ka = rms_norm(k[:, :a1], k_gain[0]).astype(f32)
    qb = rms_norm(q[:, a1:a2], q_gain[1]).astype(f32)
    kb = rms_norm(k[:, a1:a2], k_gain[1]).astype(f32)
    oa = dilated_attention(qa, ka, v[:, :a1].astype(f32), slopes_a)
    ob = moba_attention(qb, kb, v[:, a1:a2].astype(f32), slopes_b)
    oc = stick_breaking_attention(q[:, a2:].astype(f32), k[:, a2:].astype(f32), v[:, a2:].astype(f32))
    o = jnp.concatenate([oa, ob, oc], axis=1)
    o = rms_norm(o, out_gain.reshape(N_HEADS, 1, HEAD_DIM))
    o = o.transpose(0, 2, 1, 3).reshape(B, S, D).astype(h.dtype)
    return o @ w_out


def swiglu(h, w_gate, w_up, w_down):
    return (jax.nn.silu(h @ w_gate) * (h @ w_up)) @ w_down


def moe_swiglu(h, router, w_gate, w_up, w_down):
    B, S, D = h.shape
    t = h.reshape(B * S, D)
    logits = (t @ router).astype(jnp.float32)
    topv, topi = lax.top_k(logits, TOP_K)
    gates = jax.nn.softmax(topv, axis=-1)
    combine = jnp.sum(jax.nn.one_hot(topi, N_EXPERTS, dtype=jnp.float32) * gates[..., None], axis=1)
    combine = combine.astype(h.dtype)
    out = jnp.zeros_like(t)
    for e in range(N_EXPERTS):
        out = out + combine[:, e:e + 1] * swiglu(t, w_gate[e], w_up[e], w_down[e])
    return out.reshape(B, S, D)


def setup_inputs(seed: int = 0) -> dict:
    key = jax.random.key(seed)
    ks = jax.random.split(key, 16)
    D, F, E = D_MODEL, D_FF, N_EXPERTS
    nd, nm = (DEPTH + 1) // 2, DEPTH // 2

    def normal(k, shape, scale):
        return jax.random.normal(k, shape, jnp.float32) * scale

    def gain(k, shape):
        return 1.0 + 0.02 * jax.random.normal(k, shape, jnp.float32)

    return {
        "x": normal(ks[0], (BATCH, SEQ, D), 1.0),
        "attn_norm": gain(ks[1], (DEPTH, D)),
        "w_in": normal(ks[2], (DEPTH, D, 3 * D), D ** -0.5),
        "q_gain": gain(ks[3], (DEPTH, 2, HEAD_DIM)),
        "k_gain": gain(ks[4], (DEPTH, 2, HEAD_DIM)),
        "out_gain": gain(ks[5], (DEPTH, D)),
        "w_out": normal(ks[6], (DEPTH, D, D), D ** -0.5),
        "ffn_norm": gain(ks[7], (DEPTH, D)),
        "dense_w_gate": normal(ks[8], (nd, D, F), D ** -0.5),
        "dense_w_up": normal(ks[9], (nd, D, F), D ** -0.5),
        "dense_w_down": normal(ks[10], (nd, F, D), F ** -0.5),
        "moe_router": normal(ks[11], (nm, D, E), D ** -0.5),
        "moe_w_gate": normal(ks[12], (nm, E, D, F), D ** -0.5),
        "moe_w_up": normal(ks[13], (nm, E, D, F), D ** -0.5),
        "moe_w_down": normal(ks[14], (nm, E, F, D), F ** -0.5),
    }


def reference(x, attn_norm, w_in, q_gain, k_gain, out_gain, w_out, ffn_norm,
              dense_w_gate, dense_w_up, dense_w_down,
              moe_router, moe_w_gate, moe_w_up, moe_w_down):
    slopes_a, slopes_b = alibi_slopes()
    h = x
    for layer in range(DEPTH):
        a = rms_norm(h, attn_norm[layer])
        h = h + hybrid_mixer(a, w_in[layer], q_gain[layer], k_gain[layer], out_gain[layer],
                             w_out[layer], slopes_a, slopes_b)
        f = rms_norm(h, ffn_norm[layer])
        i = layer // 2
        if layer % 2 == 0:
            h = h + swiglu(f, dense_w_gate[i], dense_w_up[i], dense_w_down[i])
        else:
            h = h + moe_swiglu(f, moe_router[i], moe_w_gate[i], moe_w_up[i], moe_w_down[i])
    return h
```

```python
import functools

import numpy as np
import jax
import jax.numpy as jnp
from jax import lax
from jax.experimental import pallas as pl
from jax.experimental.pallas import tpu as pltpu

HEAD_DIM = 128
N_HEADS = 16
HEADS_DILATED = 6
HEADS_MOBA = 5
HEADS_STICK = 5
DILATED_PATTERNS = ((128, 1), (512, 4), (2048, 16))
MOBA_BLOCK = 256
MOBA_TOPK = 3
N_EXPERTS = 8
EPS = 1e-6

LANES = 128
VMEM_LIMIT = 56 * 1024 * 1024
ATTN_TILE = 256
NEG_INF = float("-inf")

F32 = jnp.float32
BF16 = jnp.bfloat16


def _params(sem, vmem=VMEM_LIMIT):
    return pltpu.CompilerParams(dimension_semantics=sem, vmem_limit_bytes=vmem)


def _dot(a, b):
    return jnp.dot(a, b, preferred_element_type=F32)


def _dot_nt(a, b):
    return lax.dot_general(a, b, (((1,), (1,)), ((), ())), preferred_element_type=F32)


def _rms(x, gain):
    ms = jnp.mean(x * x, axis=-1, keepdims=True)
    return x * lax.rsqrt(ms + EPS) * gain


def _split_bf16(x):
    hi = x.astype(BF16)
    lo = (x - hi.astype(F32)).astype(BF16)
    return hi, lo


def _qkv_kernel(x_ref, g_ref, w_ref, hg_ref, hf_ref, o_ref, a_ref):
    @pl.when(pl.program_id(1) == 0)
    def _():
        a_ref[...] = _rms(x_ref[...], g_ref[...]).astype(BF16)

    y = _dot(a_ref[...], w_ref[...])
    for c in range(y.shape[1] // HEAD_DIM):
        cols = slice(c * HEAD_DIM, (c + 1) * HEAD_DIM)
        blk = y[:, cols]
        normed = _rms(blk, hg_ref[:, cols])
        o_ref[:, cols] = jnp.where(hf_ref[:, cols] > 0, normed, blk).astype(o_ref.dtype)


def _qkv_proj(h, gain, w_bf16, head_gain, head_flag, *, tm=512, tn=768):
    T, D = h.shape
    N = w_bf16.shape[1]
    return pl.pallas_call(
        _qkv_kernel,
        grid=(T // tm, N // tn),
        in_specs=[
            pl.BlockSpec((tm, D), lambda i, j: (i, 0)),
            pl.BlockSpec((1, D), lambda i, j: (0, 0)),
            pl.BlockSpec((D, tn), lambda i, j: (0, j)),
            pl.BlockSpec((1, tn), lambda i, j: (0, j)),
            pl.BlockSpec((1, tn), lambda i, j: (0, j)),
        ],
        out_specs=pl.BlockSpec((tm, tn), lambda i, j: (i, j)),
        out_shape=jax.ShapeDtypeStruct((T, N), BF16),
        scratch_shapes=[pltpu.VMEM((tm, D), BF16)],
        compiler_params=_params(("parallel", "arbitrary")),
        name="qkv_proj",
    )(h, gain.reshape(1, D), w_bf16, head_gain, head_flag)


def _attn_specs(S, n_q_tiles, head0, tq):
    q_spec = pl.BlockSpec((tq, HEAD_DIM), lambda b, h, i, *_: (b * n_q_tiles + i, head0 + h))
    k_spec = pl.BlockSpec((S, HEAD_DIM), lambda b, h, i, *_: (b, N_HEADS + head0 + h))
    v_spec = pl.BlockSpec((S, HEAD_DIM), lambda b, h, i, *_: (b, 2 * N_HEADS + head0 + h))
    g_spec = pl.BlockSpec((1, HEAD_DIM), lambda b, h, i, *_: (0, head0 + h))
    o_spec = pl.BlockSpec((tq, HEAD_DIM), lambda b, h, i, *_: (b * n_q_tiles + i, h))
    return q_spec, k_spec, v_spec, g_spec, o_spec


def _key_block(ref, n, tq):
    return ref[pl.ds(pl.multiple_of(n * tq, tq), tq), :]


def _softmax_first(s, vb):
    m = jnp.max(s, axis=-1, keepdims=True)
    p = jnp.exp(s - m)
    return m, jnp.sum(p, axis=-1, keepdims=True), _dot(p.astype(BF16), vb)


def _softmax_next(s, vb, m, l, acc):
    m_new = jnp.maximum(m, jnp.max(s, axis=-1, keepdims=True))
    alpha = jnp.exp(m - m_new)
    p = jnp.exp(s - m_new)
    l = alpha * l + jnp.sum(p, axis=-1, keepdims=True)
    acc = alpha * acc + _dot(p.astype(BF16), vb)
    return m_new, l, acc


def _dilated_kernel(q_ref, k_ref, v_ref, g_ref, bias_ref, o_ref, *, tq, n_back, scale):
    i = pl.program_id(2)
    q = q_ref[...]
    s = _dot_nt(q, _key_block(k_ref, i, tq)) * scale + bias_ref[0, 0]
    state = _softmax_first(s, _key_block(v_ref, i, tq))

    def body(d, state):
        n = i - d
        s = _dot_nt(q, _key_block(k_ref, n, tq)) * scale + bias_ref[0, d]
        return _softmax_next(s, _key_block(v_ref, n, tq), *state)

    m, l, acc = lax.fori_loop(1, jnp.minimum(i, n_back) + 1, body, state)
    o_ref[...] = _rms(acc / l, g_ref[...]).astype(o_ref.dtype)


def _dilated_bias(slopes, tq, n_back):
    d = jnp.arange(n_back + 1, dtype=jnp.int32)[:, None, None]
    r = jnp.arange(tq, dtype=jnp.int32)[None, :, None]
    c = jnp.arange(tq, dtype=jnp.int32)[None, None, :]
    dist = d * tq + r - c
    mult = jnp.zeros(dist.shape, F32)
    for window, dilation in DILATED_PATTERNS:
        hit = (dist >= 0) & (dist <= window) & (dist % dilation == 0)
        mult = mult + hit.astype(F32)
    logm = jnp.where(mult > 0, jnp.log(jnp.maximum(mult, 1.0)), NEG_INF)
    return logm[None] - slopes[:, None, None, None] * dist.astype(F32)[None]


def _dilated_attention(qkv, out_gain, slopes, B, S, *, tq=ATTN_TILE):
    nq = S // tq
    max_window = max(w for w, _ in DILATED_PATTERNS)
    n_back = -(-max_window // tq)
    bias = _dilated_bias(slopes, tq, n_back)
    q_spec, k_spec, v_spec, g_spec, o_spec = _attn_specs(S, nq, 0, tq)
    bias_spec = pl.BlockSpec((1, n_back + 1, tq, tq), lambda b, h, i: (h, 0, 0, 0))
    return pl.pallas_call(
        functools.partial(_dilated_kernel, tq=tq, n_back=n_back, scale=HEAD_DIM ** -0.5),
        grid=(B, HEADS_DILATED, nq),
        in_specs=[q_spec, k_spec, v_spec, g_spec, bias_spec],
        out_specs=o_spec,
        out_shape=jax.ShapeDtypeStruct((B * S, HEADS_DILATED * HEAD_DIM), BF16),
        compiler_params=_params(("parallel", "parallel", "arbitrary")),
        name="dilated_attn",
    )(qkv, qkv, qkv, out_gain, bias)


def _moba_kernel(slopes_ref, q_ref, k_ref, v_ref, g_ref, o_ref, kmean_ref, selb_ref,
                 *, tq, n_blocks, scale):
    h = pl.program_id(1)
    i = pl.program_id(2)
    slope = slopes_ref[h]

    @pl.when(i == 0)
    def _():
        kmean_ref[...] = jnp.zeros_like(kmean_ref)
        for n in range(n_blocks):
            kb = k_ref[n * tq:(n + 1) * tq, :].astype(F32)
            kmean_ref[n:n + 1, :] = jnp.mean(kb, axis=0, keepdims=True)

    q = q_ref[...]
    km_hi, km_lo = _split_bf16(kmean_ref[...])
    gate = _dot_nt(q, km_hi) + _dot_nt(q, km_lo)
    lane = lax.broadcasted_iota(jnp.int32, gate.shape, 1)
    lane_f = lane.astype(F32)
    g = jnp.where(lane < i, gate, NEG_INF)
    sel = jnp.zeros(gate.shape, jnp.bool_)
    for _ in range(MOBA_TOPK):
        top = jnp.max(g, axis=-1, keepdims=True)
        is_top = (g == top) & (top > NEG_INF)
        first = jnp.min(jnp.where(is_top, lane_f, float(LANES)), axis=-1, keepdims=True)
        pick = lane_f == first
        sel = sel | pick
        g = jnp.where(pick, NEG_INF, g)
    sel_bias = jnp.where(sel, 0.0, NEG_INF)
    for n in range(n_blocks):
        selb_ref[n] = jnp.broadcast_to(sel_bias[:, n:n + 1], (tq, tq))

    r = lax.broadcasted_iota(jnp.int32, (tq, tq), 0)
    c = lax.broadcasted_iota(jnp.int32, (tq, tq), 1)
    rc = r - c
    s = _dot_nt(q, _key_block(k_ref, i, tq)) * scale - slope * rc.astype(F32)
    s = jnp.where(rc >= 0, s, NEG_INF)
    state = _softmax_first(s, _key_block(v_ref, i, tq))

    def body(n, state):
        dist = (rc + (i - n) * tq).astype(F32)
        s = _dot_nt(q, _key_block(k_ref, n, tq)) * scale - slope * dist + selb_ref[n]
        return _softmax_next(s, _key_block(v_ref, n, tq), *state)

    m, l, acc = lax.fori_loop(0, i, body, state)
    o_ref[...] = _rms(acc / l, g_ref[...]).astype(o_ref.dtype)


def _moba_attention(qkv, out_gain, slopes, B, S, *, tq=MOBA_BLOCK):
    assert S % tq == 0 and S // tq <= LANES
    nq = S // tq
    head0 = HEADS_DILATED
    q_spec, k_spec, v_spec, g_spec, o_spec = _attn_specs(S, nq, head0, tq)
    return pl.pallas_call(
        functools.partial(_moba_kernel, tq=tq, n_blocks=nq, scale=HEAD_DIM ** -0.5),
        grid=(B, HEADS_MOBA, nq),
        in_specs=[pl.BlockSpec(memory_space=pltpu.SMEM), q_spec, k_spec, v_spec, g_spec],
        out_specs=o_spec,
        scratch_shapes=[pltpu.VMEM((LANES, HEAD_DIM), F32), pltpu.VMEM((nq, tq, tq), F32)],
        out_shape=jax.ShapeDtypeStruct((B * S, HEADS_MOBA * HEAD_DIM), BF16),
        compiler_params=_params(("arbitrary", "arbitrary", "arbitrary")),
        name="moba_attn",
    )(slopes, qkv, qkv, qkv, out_gain)


def _stick_kernel(q_ref, k_ref, v_ref, g_ref, o_ref, *, tq, scale):
    i = pl.program_id(2)
    q = q_ref[...]
    r = lax.broadcasted_iota(jnp.int32, (tq, tq), 0)
    c = lax.broadcasted_iota(jnp.int32, (tq, tq), 1)
    past = c < r
    later = (r > c).astype(BF16)

    def block(n, carry, acc, diagonal):
        z = _dot_nt(q, _key_block(k_ref, n, tq)) * scale
        log_1m = jnp.minimum(-z, 0.0) - jnp.log1p(jnp.exp(-jnp.abs(z)))
        if diagonal:
            log_1m = jnp.where(past, log_1m, 0.0)
        hi, lo = _split_bf16(log_1m)
        after = _dot(hi, later) + _dot(lo, later) + carry
        a = jnp.exp(z + log_1m + after) if not diagonal else jnp.where(
            past, jnp.exp(z + log_1m + after), 0.0)
        acc = acc + _dot(a.astype(BF16), _key_block(v_ref, n, tq))
        carry = carry + jnp.sum(log_1m, axis=-1, keepdims=True)
        return carry, acc

    carry, acc = block(i, jnp.zeros((tq, 1), F32), jnp.zeros((tq, HEAD_DIM), F32), True)

    def body(t, state):
        return block(i - 1 - t, *state, False)

    carry, acc = lax.fori_loop(0, i, body, (carry, acc))
    o_ref[...] = _rms(acc, g_ref[...]).astype(o_ref.dtype)


def _stick_attention(qkv, out_gain, B, S, *, tq=ATTN_TILE):
    nq = S // tq
    head0 = HEADS_DILATED + HEADS_MOBA
    q_spec, k_spec, v_spec, g_spec, o_spec = _attn_specs(S, nq, head0, tq)
    return pl.pallas_call(
        functools.partial(_stick_kernel, tq=tq, scale=HEAD_DIM ** -0.5),
        grid=(B, HEADS_STICK, nq),
        in_specs=[q_spec, k_spec, v_spec, g_spec],
        out_specs=o_spec,
        out_shape=jax.ShapeDtypeStruct((B * S, HEADS_STICK * HEAD_DIM), BF16),
        compiler_params=_params(("parallel", "parallel", "arbitrary")),
        name="stick_attn",
    )(qkv, qkv, qkv, out_gain)


def _oproj_kernel(oa_ref, ob_ref, oc_ref, w_ref, h_ref, g_ref, hn_ref, f_ref):
    ka, kb = oa_ref.shape[1], ob_ref.shape[1]
    y = _dot(oa_ref[...], w_ref[0:ka, :])
    y = y + _dot(ob_ref[...], w_ref[ka:ka + kb, :])
    y = y + _dot(oc_ref[...], w_ref[ka + kb:, :])
    hn = h_ref[...] + y
    hn_ref[...] = hn
    f_ref[...] = _rms(hn, g_ref[...]).astype(f_ref.dtype)


def _out_proj(oa, ob, oc, w_bf16, h, gain, *, tm=256):
    T, D = h.shape
    row = lambda i: (i, 0)
    fixed = lambda i: (0, 0)
    return pl.pallas_call(
        _oproj_kernel,
        grid=(T // tm,),
        in_specs=[
            pl.BlockSpec((tm, oa.shape[1]), row),
            pl.BlockSpec((tm, ob.shape[1]), row),
            pl.BlockSpec((tm, oc.shape[1]), row),
            pl.BlockSpec((D, D), fixed),
            pl.BlockSpec((tm, D), row),
            pl.BlockSpec((1, D), fixed),
        ],
        out_specs=[pl.BlockSpec((tm, D), row), pl.BlockSpec((tm, D), row)],
        out_shape=[jax.ShapeDtypeStruct((T, D), F32), jax.ShapeDtypeStruct((T, D), BF16)],
        compiler_params=_params(("parallel",)),
        name="out_proj",
    )(oa, ob, oc, w_bf16, h, gain.reshape(1, D))


def _swiglu_block(f, wg_ref, wu_ref, wd_ref):
    g = _dot(f, wg_ref[...])
    u = _dot(f, wu_ref[...])
    a = (g * jax.nn.sigmoid(g) * u).astype(BF16)
    return _dot(a, wd_ref[...])


def _ffn_kernel(f_ref, wg_ref, wu_ref, wd_ref, h_ref, o_ref, acc_ref):
    j = pl.program_id(1)

    @pl.when(j == 0)
    def _():
        acc_ref[...] = h_ref[...]

    acc_ref[...] += _swiglu_block(f_ref[...], wg_ref, wu_ref, wd_ref)

    @pl.when(j == pl.num_programs(1) - 1)
    def _():
        o_ref[...] = acc_ref[...]


def _ffn_dense(f, h, wg, wu, wd, *, tm=512, tf=512):
    T, D = h.shape
    F = wg.shape[1]
    return pl.pallas_call(
        _ffn_kernel,
        grid=(T // tm, F // tf),
        in_specs=[
            pl.BlockSpec((tm, D), lambda i, j: (i, 0)),
            pl.BlockSpec((D, tf), lambda i, j: (0, j)),
            pl.BlockSpec((D, tf), lambda i, j: (0, j)),
            pl.BlockSpec((tf, D), lambda i, j: (j, 0)),
            pl.BlockSpec((tm, D), lambda i, j: (i, 0)),
        ],
        out_specs=pl.BlockSpec((tm, D), lambda i, j: (i, 0)),
        out_shape=jax.ShapeDtypeStruct((T, D), F32),
        scratch_shapes=[pltpu.VMEM((tm, D), F32)],
        compiler_params=_params(("parallel", "arbitrary")),
        name="ffn_dense",
    )(f, wg, wu, wd, h)


def _router_kernel(f_ref, w_ref, comb_ref):
    logits = _dot(f_ref[...], w_ref[...])
    lane = lax.broadcasted_iota(jnp.int32, logits.shape, 1).astype(F32)
    g = jnp.where(lane < N_EXPERTS, logits, NEG_INF)
    picks, tops = [], []
    for _ in range(2):
        top = jnp.max(g, axis=-1, keepdims=True)
        first = jnp.min(jnp.where(g == top, lane, float(LANES)), axis=-1, keepdims=True)
        pick = lane == first
        g = jnp.where(pick, NEG_INF, g)
        picks.append(pick)
        tops.append(top)
    e2 = jnp.exp(tops[1] - tops[0])
    denom = 1.0 + e2
    comb_ref[...] = jnp.where(picks[0], 1.0 / denom, 0.0) + jnp.where(picks[1], e2 / denom, 0.0)


def _router(f, router_w, *, tm=512):
    T, D = f.shape
    w = jnp.zeros((D, LANES), BF16).at[:, :N_EXPERTS].set(router_w.astype(BF16))
    return pl.pallas_call(
        _router_kernel,
        grid=(T // tm,),
        in_specs=[pl.BlockSpec((tm, D), lambda i: (i, 0)), pl.BlockSpec((D, LANES), lambda i: (0, 0))],
        out_specs=pl.BlockSpec((tm, LANES), lambda i: (i, 0)),
        out_shape=jax.ShapeDtypeStruct((T, LANES), F32),
        compiler_params=_params(("parallel",)),
        name="moe_router",
    )(f, w)


def _moe_kernel(f_ref, comb_ref, wg_ref, wu_ref, wd_ref, h_ref, o_ref, acc_ref):
    e = pl.program_id(1)
    j = pl.program_id(2)

    @pl.when((e == 0) & (j == 0))
    def _():
        acc_ref[...] = h_ref[...]

    comb = comb_ref[...]
    lane = lax.broadcasted_iota(jnp.int32, comb.shape, 1)
    weight = jnp.sum(jnp.where(lane == e, comb, 0.0), axis=-1, keepdims=True)
    acc_ref[...] += weight * _swiglu_block(f_ref[...], wg_ref.at[0], wu_ref.at[0], wd_ref.at[0])

    @pl.when((e == pl.num_programs(1) - 1) & (j == pl.num_programs(2) - 1))
    def _():
        o_ref[...] = acc_ref[...]


def _moe_dense(f, comb, h, wg, wu, wd, *, tm=512, tf=512):
    T, D = h.shape
    E, _, F = wg.shape
    return pl.pallas_call(
        _moe_kernel,
        grid=(T // tm, E, F // tf),
        in_specs=[
            pl.BlockSpec((tm, D), lambda i, e, j: (i, 0)),
            pl.BlockSpec((tm, LANES), lambda i, e, j: (i, 0)),
            pl.BlockSpec((1, D, tf), lambda i, e, j: (e, 0, j)),
            pl.BlockSpec((1, D, tf), lambda i, e, j: (e, 0, j)),
            pl.BlockSpec((1, tf, D), lambda i, e, j: (e, j, 0)),
            pl.BlockSpec((tm, D), lambda i, e, j: (i, 0)),
        ],
        out_specs=pl.BlockSpec((tm, D), lambda i, e, j: (i, 0)),
        out_shape=jax.ShapeDtypeStruct((T, D), F32),
        scratch_shapes=[pltpu.VMEM((tm, D), F32)],
        compiler_params=_params(("parallel", "arbitrary", "arbitrary")),
        name="moe_ffn",
    )(f, comb, wg, wu, wd, h)


def _alibi_slopes():
    n = HEADS_DILATED + HEADS_MOBA
    s = jnp.asarray(2.0 ** (-8.0 * np.arange(1, n + 1) / n), dtype=F32)
    return s[:HEADS_DILATED], s[HEADS_DILATED:]


def _head_norm_rows(q_gain, k_gain, D):
    a = HEADS_DILATED * HEAD_DIM
    b = HEADS_MOBA * HEAD_DIM
    rest = D - a - b

    def row(g):
        return jnp.concatenate([jnp.tile(g[0], HEADS_DILATED), jnp.tile(g[1], HEADS_MOBA), jnp.ones((rest,), F32)])

    flag = jnp.concatenate([jnp.ones((a + b,), F32), jnp.zeros((rest,), F32)])
    gain = jnp.concatenate([row(q_gain), row(k_gain), jnp.ones((D,), F32)])
    flags = jnp.concatenate([flag, flag, jnp.zeros((D,), F32)])
    return gain.reshape(1, 3 * D), flags.reshape(1, 3 * D)


def kernel(x, attn_norm, w_in, q_gain, k_gain, out_gain, w_out, ffn_norm, dense_w_gate, dense_w_up, dense_w_down, moe_router, moe_w_gate, moe_w_up, moe_w_down):
    B, S, D = x.shape
    depth = w_in.shape[0]
    slopes_a, slopes_b = _alibi_slopes()
    h = x.reshape(B * S, D)
    for layer in range(depth):
        head_gain, head_flag = _head_norm_rows(q_gain[layer], k_gain[layer], D)
        qkv = _qkv_proj(h, attn_norm[layer], w_in[layer].astype(BF16), head_gain, head_flag)
        og = out_gain[layer].reshape(1, D)
        oa = _dilated_attention(qkv, og, slopes_a, B, S)
        ob = _moba_attention(qkv, og, slopes_b, B, S)
        oc = _stick_attention(qkv, og, B, S)
        h, f = _out_proj(oa, ob, oc, w_out[layer].astype(BF16), h, ffn_norm[layer])
        i = layer // 2
        if layer % 2 == 0:
            h = _ffn_dense(f, h, dense_w_gate[i].astype(BF16), dense_w_up[i].astype(BF16),
                           dense_w_down[i].astype(BF16))
        else:
            comb = _router(f, moe_router[i])
            h = _moe_dense(f, comb, h, moe_w_gate[i].astype(BF16), moe_w_up[i].astype(BF16),
                           moe_w_down[i].astype(BF16))
    return h.reshape(B, S, D)
```

```python
import functools

import numpy as np
import jax
import jax.numpy as jnp
from jax import lax
from jax.experimental import pallas as pl
from jax.experimental.pallas import tpu as pltpu

HEAD_DIM = 128
N_HEADS = 16
HEADS_DILATED = 6
HEADS_MOBA = 5
HEADS_STICK = 5
DILATED_PATTERNS = ((128, 1), (512, 4), (2048, 16))
MOBA_BLOCK = 256
MOBA_TOPK = 3
N_EXPERTS = 8
EPS = 1e-6

LANES = 128
VMEM_LIMIT = 56 * 1024 * 1024
MOE_VMEM_LIMIT = 60 * 1024 * 1024
ATTN_TILE = 256
NEG_INF = float("-inf")

F32 = jnp.float32
BF16 = jnp.bfloat16


def _params(sem, vmem=VMEM_LIMIT):
    return pltpu.CompilerParams(dimension_semantics=sem, vmem_limit_bytes=vmem)


def _dot(a, b):
    return jnp.dot(a, b, preferred_element_type=F32)


def _dot_nt(a, b):
    return lax.dot_general(a, b, (((1,), (1,)), ((), ())), preferred_element_type=F32)


def _rms(x, gain):
    ms = jnp.mean(x * x, axis=-1, keepdims=True)
    return x * lax.rsqrt(ms + EPS) * gain


def _split_bf16(x):
    hi = x.astype(BF16)
    lo = (x - hi.astype(F32)).astype(BF16)
    return hi, lo


def _qkv_kernel(x_ref, g_ref, w_ref, hg_ref, hf_ref, o_ref, a_ref):
    @pl.when(pl.program_id(1) == 0)
    def _():
        a_ref[...] = _rms(x_ref[...], g_ref[...]).astype(BF16)

    y = _dot(a_ref[...], w_ref[...])
    for c in range(y.shape[1] // HEAD_DIM):
        cols = slice(c * HEAD_DIM, (c + 1) * HEAD_DIM)
        blk = y[:, cols]
        normed = _rms(blk, hg_ref[:, cols])
        o_ref[:, cols] = jnp.where(hf_ref[:, cols] > 0, normed, blk).astype(o_ref.dtype)


def _qkv_proj(h, gain, w_bf16, head_gain, head_flag, *, tm=512, tn=768):
    T, D = h.shape
    N = w_bf16.shape[1]
    return pl.pallas_call(
        _qkv_kernel,
        grid=(T // tm, N // tn),
        in_specs=[
            pl.BlockSpec((tm, D), lambda i, j: (i, 0)),
            pl.BlockSpec((1, D), lambda i, j: (0, 0)),
            pl.BlockSpec((D, tn), lambda i, j: (0, j)),
            pl.BlockSpec((1, tn), lambda i, j: (0, j)),
            pl.BlockSpec((1, tn), lambda i, j: (0, j)),
        ],
        out_specs=pl.BlockSpec((tm, tn), lambda i, j: (i, j)),
        out_shape=jax.ShapeDtypeStruct((T, N), BF16),
        scratch_shapes=[pltpu.VMEM((tm, D), BF16)],
        compiler_params=_params(("parallel", "arbitrary")),
        name="qkv_proj",
    )(h, gain.reshape(1, D), w_bf16, head_gain, head_flag)


def _attn_specs(S, n_q_tiles, head0, tq):
    q_spec = pl.BlockSpec((tq, HEAD_DIM), lambda b, h, i, *_: (b * n_q_tiles + i, head0 + h))
    k_spec = pl.BlockSpec((S, HEAD_DIM), lambda b, h, i, *_: (b, N_HEADS + head0 + h))
    v_spec = pl.BlockSpec((S, HEAD_DIM), lambda b, h, i, *_: (b, 2 * N_HEADS + head0 + h))
    g_spec = pl.BlockSpec((1, HEAD_DIM), lambda b, h, i, *_: (0, head0 + h))
    o_spec = pl.BlockSpec((tq, HEAD_DIM), lambda b, h, i, *_: (b * n_q_tiles + i, h))
    return q_spec, k_spec, v_spec, g_spec, o_spec


def _key_rows(ref, first_block, n_blocks, tq):
    return ref[pl.ds(pl.multiple_of(first_block * tq, tq), n_blocks * tq), :]


def _lane_blocks(x, tq):
    return [x[:, w * tq:(w + 1) * tq] for w in range(x.shape[1] // tq)]


def _softmax_blocks(scores, v_rows, state):
    m_blk = jnp.max(functools.reduce(jnp.maximum, scores), axis=-1, keepdims=True)
    m_new = m_blk if state is None else jnp.maximum(state[0], m_blk)
    p = [jnp.exp(s - m_new) for s in scores]
    l_blk = jnp.sum(functools.reduce(jnp.add, p), axis=-1, keepdims=True)
    pv = _dot(jnp.concatenate([x.astype(BF16) for x in p], axis=1), v_rows)
    if state is None:
        return m_new, l_blk, pv
    m, l, acc = state
    alpha = jnp.exp(m - m_new)
    return m_new, alpha * l + l_blk, alpha * acc + pv


def _dilated_kernel(q_ref, k_ref, v_ref, g_ref, bias_ref, o_ref, *, tq, n_win, scale):
    i = pl.program_id(2)
    before = jnp.minimum(i, n_win - 1)
    q = q_ref[...]
    s_all = _dot_nt(q, _key_rows(k_ref, i - before, n_win, tq))
    scores = [s * scale + bias_ref[0, before + n_win - 1 - w]
              for w, s in enumerate(_lane_blocks(s_all, tq))]
    _, l, acc = _softmax_blocks(scores, _key_rows(v_ref, i - before, n_win, tq), None)
    o_ref[...] = _rms(acc / l, g_ref[...]).astype(o_ref.dtype)


def _dilated_bias(slopes, tq, n_back, n_win):
    d = jnp.arange(n_back + n_win, dtype=jnp.int32)[:, None, None] - (n_win - 1)
    r = jnp.arange(tq, dtype=jnp.int32)[None, :, None]
    c = jnp.arange(tq, dtype=jnp.int32)[None, None, :]
    dist = d * tq + r - c
    mult = jnp.zeros(dist.shape, F32)
    for window, dilation in DILATED_PATTERNS:
        hit = (dist >= 0) & (dist <= window) & (dist % dilation == 0)
        mult = mult + hit.astype(F32)
    logm = jnp.where(mult > 0, jnp.log(jnp.maximum(mult, 1.0)), NEG_INF)
    return logm[None] - slopes[:, None, None, None] * dist.astype(F32)[None]


def _dilated_attention(qkv, out_gain, slopes, B, S, *, tq=ATTN_TILE):
    nq = S // tq
    max_window = max(w for w, _ in DILATED_PATTERNS)
    n_back = -(-max_window // tq)
    n_win = min(n_back + 1, nq)
    bias = _dilated_bias(slopes, tq, n_back, n_win)
    q_spec, k_spec, v_spec, g_spec, o_spec = _attn_specs(S, nq, 0, tq)
    bias_spec = pl.BlockSpec((1, n_back + n_win, tq, tq), lambda b, h, i: (h, 0, 0, 0))
    return pl.pallas_call(
        functools.partial(_dilated_kernel, tq=tq, n_win=n_win, scale=HEAD_DIM ** -0.5),
        grid=(B, HEADS_DILATED, nq),
        in_specs=[q_spec, k_spec, v_spec, g_spec, bias_spec],
        out_specs=o_spec,
        out_shape=jax.ShapeDtypeStruct((B * S, HEADS_DILATED * HEAD_DIM), BF16),
        compiler_params=_params(("parallel", "parallel", "arbitrary")),
        name="dilated_attn",
    )(qkv, qkv, qkv, out_gain, bias)


def _moba_kernel(slopes_ref, q_ref, k_ref, v_ref, g_ref, o_ref, kmean_ref, bias_ref,
                 *, tq, n_blocks, group, scale):
    h = pl.program_id(1)
    i = pl.program_id(2)
    slope = slopes_ref[h]

    @pl.when(i == 0)
    def _():
        kmean_ref[...] = jnp.zeros_like(kmean_ref)
        for n in range(n_blocks):
            kb = k_ref[n * tq:(n + 1) * tq, :].astype(F32)
            kmean_ref[n:n + 1, :] = jnp.mean(kb, axis=0, keepdims=True)

    q = q_ref[...]
    km_hi, km_lo = _split_bf16(kmean_ref[...])
    gate = _dot_nt(q, km_hi) + _dot_nt(q, km_lo)
    lane = lax.broadcasted_iota(jnp.int32, gate.shape, 1)
    lane_f = lane.astype(F32)
    g = jnp.where(lane < i, gate, NEG_INF)
    sel = jnp.zeros(gate.shape, jnp.bool_)
    for _ in range(MOBA_TOPK):
        top = jnp.max(g, axis=-1, keepdims=True)
        is_top = (g == top) & (top > NEG_INF)
        first = jnp.min(jnp.where(is_top, lane_f, float(LANES)), axis=-1, keepdims=True)
        pick = lane_f == first
        sel = sel | pick
        g = jnp.where(pick, NEG_INF, g)
    sel_bias = jnp.where(sel, 0.0, NEG_INF)

    r = lax.broadcasted_iota(jnp.int32, (tq, tq), 0)
    c = lax.broadcasted_iota(jnp.int32, (tq, tq), 1)
    rc = r - c
    own_group = i // group
    for n in range(n_blocks):
        @pl.when(n < (own_group + 1) * group)
        def _():
            past = jnp.broadcast_to(sel_bias[:, n:n + 1], (tq, tq))
            own = jnp.where((n == i) & (rc >= 0), 0.0, NEG_INF)
            alibi = slope * (rc + (i - n) * tq).astype(F32)
            bias_ref[n] = jnp.where(n < i, past, own) - alibi

    def attend(grp, state):
        s_all = _dot_nt(q, _key_rows(k_ref, grp * group, group, tq))
        scores = [s * scale + bias_ref[grp * group + w] for w, s in enumerate(_lane_blocks(s_all, tq))]
        return _softmax_blocks(scores, _key_rows(v_ref, grp * group, group, tq), state)

    state = attend(own_group, None)
    _, l, acc = lax.fori_loop(0, own_group, attend, state)
    o_ref[...] = _rms(acc / l, g_ref[...]).astype(o_ref.dtype)


def _moba_attention(qkv, out_gain, slopes, B, S, *, tq=MOBA_BLOCK, group=4):
    nq = S // tq
    assert S % tq == 0 and nq <= LANES and nq % group == 0
    head0 = HEADS_DILATED
    q_spec, k_spec, v_spec, g_spec, o_spec = _attn_specs(S, nq, head0, tq)
    return pl.pallas_call(
        functools.partial(_moba_kernel, tq=tq, n_blocks=nq, group=group, scale=HEAD_DIM ** -0.5),
        grid=(B, HEADS_MOBA, nq),
        in_specs=[pl.BlockSpec(memory_space=pltpu.SMEM), q_spec, k_spec, v_spec, g_spec],
        out_specs=o_spec,
        scratch_shapes=[pltpu.VMEM((LANES, HEAD_DIM), F32), pltpu.VMEM((nq, tq, tq), F32)],
        out_shape=jax.ShapeDtypeStruct((B * S, HEADS_MOBA * HEAD_DIM), BF16),
        compiler_params=_params(("arbitrary", "arbitrary", "arbitrary")),
        name="moba_attn",
    )(slopes, qkv, qkv, qkv, out_gain)


def _stick_kernel(q_ref, k_ref, v_ref, g_ref, o_ref, *, tq, group, scale):
    i = pl.program_id(2)
    q = q_ref[...]
    r = lax.broadcasted_iota(jnp.int32, (tq, tq), 0)
    c = lax.broadcasted_iota(jnp.int32, (tq, tq), 1)
    rc = r - c
    later = (r > c).astype(BF16)

    def attend(grp, carry, acc, masked):
        z_all = _dot_nt(q, _key_rows(k_ref, grp * group, group, tq)) * scale
        z_blocks = _lane_blocks(z_all, tq)
        weights = [None] * group
        for w in reversed(range(group)):
            z = z_blocks[w]
            log_1m = jnp.minimum(-z, 0.0) - jnp.log(1.0 + jnp.exp(-jnp.abs(z)))
            if masked:
                past = rc > (grp * group + w - i) * tq
                log_1m = jnp.where(past, log_1m, 0.0)
            hi, lo = _split_bf16(log_1m)
            after = _dot(hi, later) + _dot(lo, later) + carry
            a = jnp.exp(z + log_1m + after)
            if masked:
                a = jnp.where(past, a, 0.0)
            weights[w] = a.astype(BF16)
            carry = carry + jnp.sum(log_1m, axis=-1, keepdims=True)
        acc = acc + _dot(jnp.concatenate(weights, axis=1), _key_rows(v_ref, grp * group, group, tq))
        return carry, acc

    own_group = i // group
    state = attend(own_group, jnp.zeros((tq, 1), F32), jnp.zeros((tq, HEAD_DIM), F32), True)

    def body(t, state):
        return attend(own_group - 1 - t, *state, False)

    _, acc = lax.fori_loop(0, own_group, body, state)
    o_ref[...] = _rms(acc, g_ref[...]).astype(o_ref.dtype)


def _stick_attention(qkv, out_gain, B, S, *, tq=ATTN_TILE, group=2):
    nq = S // tq
    assert nq % group == 0
    head0 = HEADS_DILATED + HEADS_MOBA
    q_spec, k_spec, v_spec, g_spec, o_spec = _attn_specs(S, nq, head0, tq)
    return pl.pallas_call(
        functools.partial(_stick_kernel, tq=tq, group=group, scale=HEAD_DIM ** -0.5),
        grid=(B, HEADS_STICK, nq),
        in_specs=[q_spec, k_spec, v_spec, g_spec],
        out_specs=o_spec,
        out_shape=jax.ShapeDtypeStruct((B * S, HEADS_STICK * HEAD_DIM), BF16),
        compiler_params=_params(("parallel", "parallel", "arbitrary")),
        name="stick_attn",
    )(qkv, qkv, qkv, out_gain)


def _oproj_kernel(oa_ref, ob_ref, oc_ref, w_ref, h_ref, g_ref, hn_ref, f_ref):
    ka, kb = oa_ref.shape[1], ob_ref.shape[1]
    y = _dot(oa_ref[...], w_ref[0:ka, :])
    y = y + _dot(ob_ref[...], w_ref[ka:ka + kb, :])
    y = y + _dot(oc_ref[...], w_ref[ka + kb:, :])
    hn = h_ref[...] + y
    hn_ref[...] = hn
    f_ref[...] = _rms(hn, g_ref[...]).astype(f_ref.dtype)


def _out_proj(oa, ob, oc, w_bf16, h, gain, f_dtype, *, tm=256):
    T, D = h.shape
    row = lambda i: (i, 0)
    fixed = lambda i: (0, 0)
    return pl.pallas_call(
        _oproj_kernel,
        grid=(T // tm,),
        in_specs=[
            pl.BlockSpec((tm, oa.shape[1]), row),
            pl.BlockSpec((tm, ob.shape[1]), row),
            pl.BlockSpec((tm, oc.shape[1]), row),
            pl.BlockSpec((D, D), fixed),
            pl.BlockSpec((tm, D), row),
            pl.BlockSpec((1, D), fixed),
        ],
        out_specs=[pl.BlockSpec((tm, D), row), pl.BlockSpec((tm, D), row)],
        out_shape=[jax.ShapeDtypeStruct((T, D), F32), jax.ShapeDtypeStruct((T, D), f_dtype)],
        compiler_params=_params(("parallel",)),
        name="out_proj",
    )(oa, ob, oc, w_bf16, h, gain.reshape(1, D))


def _swiglu_block(f, wg_ref, wu_ref, wd_ref):
    g = _dot(f, wg_ref[...])
    u = _dot(f, wu_ref[...])
    a = (g * jax.nn.sigmoid(g) * u).astype(BF16)
    return _dot(a, wd_ref[...])


def _ffn_kernel(f_ref, wg_ref, wu_ref, wd_ref, h_ref, o_ref, acc_ref):
    j = pl.program_id(1)

    @pl.when(j == 0)
    def _():
        acc_ref[...] = h_ref[...]

    acc_ref[...] += _swiglu_block(f_ref[...], wg_ref, wu_ref, wd_ref)

    @pl.when(j == pl.num_programs(1) - 1)
    def _():
        o_ref[...] = acc_ref[...]


def _ffn_dense(f, h, wg, wu, wd, *, tm=512, tf=512):
    T, D = h.shape
    F = wg.shape[1]
    return pl.pallas_call(
        _ffn_kernel,
        grid=(T // tm, F // tf),
        in_specs=[
            pl.BlockSpec((tm, D), lambda i, j: (i, 0)),
            pl.BlockSpec((D, tf), lambda i, j: (0, j)),
            pl.BlockSpec((D, tf), lambda i, j: (0, j)),
            pl.BlockSpec((tf, D), lambda i, j: (j, 0)),
            pl.BlockSpec((tm, D), lambda i, j: (i, 0)),
        ],
        out_specs=pl.BlockSpec((tm, D), lambda i, j: (i, 0)),
        out_shape=jax.ShapeDtypeStruct((T, D), F32),
        scratch_shapes=[pltpu.VMEM((tm, D), F32)],
        compiler_params=_params(("parallel", "arbitrary")),
        name="ffn_dense",
    )(f, wg, wu, wd, h)


META_EXPERT, META_RANK, META_GATE = 0, 2, 4


def _router_kernel(f_ref, w_ref, meta_ref, count_ref):
    @pl.when(pl.program_id(0) == 0)
    def _():
        count_ref[...] = jnp.zeros_like(count_ref)

    logits = _dot(f_ref[...].astype(BF16), w_ref[...])
    tm = logits.shape[0]
    lane = lax.broadcasted_iota(jnp.int32, logits.shape, 1).astype(F32)
    g = jnp.where(lane < N_EXPERTS, logits, NEG_INF)
    picks, tops, experts = [], [], []
    for _ in range(2):
        top = jnp.max(g, axis=-1, keepdims=True)
        first = jnp.min(jnp.where(g == top, lane, float(LANES)), axis=-1, keepdims=True)
        pick = lane == first
        g = jnp.where(pick, NEG_INF, g)
        picks.append(pick)
        tops.append(top)
        experts.append(first)
    e2 = jnp.exp(tops[1] - tops[0])
    denom = 1.0 + e2
    gates = [1.0 / denom, e2 / denom]

    chosen = (picks[0] | picks[1]).astype(BF16)
    r = lax.broadcasted_iota(jnp.int32, (tm, tm), 0)
    c = lax.broadcasted_iota(jnp.int32, (tm, tm), 1)
    before = _dot((r > c).astype(BF16), chosen) + count_ref[...]
    ranks = [jnp.sum(jnp.where(p, before, 0.0), axis=-1, keepdims=True) for p in picks]
    count_ref[...] += jnp.sum(chosen.astype(F32), axis=0, keepdims=True)

    meta = jnp.zeros(logits.shape, F32)
    for base, pair in ((META_EXPERT, experts), (META_RANK, ranks), (META_GATE, gates)):
        for s in range(2):
            meta = jnp.where(lane == float(base + s), pair[s], meta)
    meta_ref[...] = meta


def _router(f, router_w, *, tm=512):
    T, D = f.shape
    w = jnp.zeros((D, LANES), BF16).at[:, :N_EXPERTS].set(router_w.astype(BF16))
    return pl.pallas_call(
        _router_kernel,
        grid=(T // tm,),
        in_specs=[pl.BlockSpec((tm, D), lambda i: (i, 0)), pl.BlockSpec((D, LANES), lambda i: (0, 0))],
        out_specs=[pl.BlockSpec((tm, LANES), lambda i: (i, 0)), pl.BlockSpec((1, LANES), lambda i: (0, 0))],
        out_shape=[jax.ShapeDtypeStruct((T, LANES), F32), jax.ShapeDtypeStruct((1, LANES), F32)],
        compiler_params=_params(("arbitrary",)),
        name="moe_router",
    )(f, w)


def _dispatch_kernel(pos_ref, f_ref, xs_init_ref, xs_ref, sem):
    del xs_init_ref
    i = pl.program_id(0)
    n_slots = pos_ref.shape[2]
    tt = n_slots // 2

    def issue(r, carry):
        src = f_ref.at[pl.ds(i * tt + r, 1)]
        for s in range(2):
            pltpu.make_async_copy(src, xs_ref.at[pl.ds(pos_ref[0, 0, 2 * r + s], 1)], sem).start()
        return carry

    lax.fori_loop(0, tt, issue, 0)
    pltpu.make_async_copy(f_ref.at[pl.ds(0, n_slots)], xs_ref.at[pl.ds(0, n_slots)], sem).wait()


def _dispatch(pos3, f, n_rows):
    T, D = f.shape
    n_tiles, _, n_slots = pos3.shape
    return pl.pallas_call(
        _dispatch_kernel,
        grid=(n_tiles,),
        in_specs=[
            pl.BlockSpec((1, 1, n_slots), lambda i: (i, 0, 0), memory_space=pltpu.SMEM),
            pl.BlockSpec(memory_space=pl.ANY),
            pl.BlockSpec(memory_space=pl.ANY),
        ],
        out_specs=pl.BlockSpec(memory_space=pl.ANY),
        out_shape=jax.ShapeDtypeStruct((n_rows, D), f.dtype),
        scratch_shapes=[pltpu.SemaphoreType.DMA(())],
        input_output_aliases={2: 0},
        compiler_params=_params(("arbitrary",)),
        name="moe_dispatch",
    )(pos3, f, jnp.zeros((n_rows, D), f.dtype))


def _moe_group_kernel(te_ref, nu_ref, x_ref, wg_ref, wu_ref, wd_ref, y_ref, xb_ref):
    del te_ref
    g = pl.program_id(0)
    j = pl.program_id(1)
    used = g < nu_ref[0]

    @pl.when(used & (j == 0))
    def _():
        xb_ref[...] = x_ref[...].astype(BF16)

    @pl.when(used)
    def _():
        x = xb_ref[...]
        gate = _dot(x, wg_ref[0].astype(BF16))
        up = _dot(x, wu_ref[0].astype(BF16))
        a = (gate * jax.nn.sigmoid(gate) * up).astype(BF16)
        part = _dot(a, wd_ref[0].astype(BF16))

        @pl.when(j == 0)
        def _():
            y_ref[...] = part

        @pl.when(j > 0)
        def _():
            y_ref[...] += part

    @pl.when(jnp.logical_not(used) & (j == 0))
    def _():
        y_ref[...] = jnp.zeros_like(y_ref)


def _moe_grouped_ffn(tile_expert, n_used, xs, wg, wu, wd, *, tmg, tf):
    P, D = xs.shape
    E, _, F = wg.shape
    J = F // tf

    def f_block(g, j, nu):
        return jnp.where(g < nu[0], j, J - 1)

    grid_spec = pltpu.PrefetchScalarGridSpec(
        num_scalar_prefetch=2,
        grid=(P // tmg, J),
        in_specs=[
            pl.BlockSpec((tmg, D), lambda g, j, te, nu: (jnp.minimum(g, nu[0] - 1), 0)),
            pl.BlockSpec((1, D, tf), lambda g, j, te, nu: (te[g], 0, f_block(g, j, nu))),
            pl.BlockSpec((1, D, tf), lambda g, j, te, nu: (te[g], 0, f_block(g, j, nu))),
            pl.BlockSpec((1, tf, D), lambda g, j, te, nu: (te[g], f_block(g, j, nu), 0)),
        ],
        out_specs=pl.BlockSpec((tmg, D), lambda g, j, te, nu: (g, 0)),
        scratch_shapes=[pltpu.VMEM((tmg, D), BF16)],
    )
    return pl.pallas_call(
        _moe_group_kernel,
        grid_spec=grid_spec,
        out_shape=jax.ShapeDtypeStruct((P, D), F32),
        compiler_params=_params(("arbitrary", "arbitrary"), vmem=MOE_VMEM_LIMIT),
        name="moe_ffn",
    )(tile_expert, n_used, xs, wg, wu, wd)


def _combine_kernel(pos_ref, y_ref, h_ref, meta_ref, o_ref, buf_ref, sem):
    tt = h_ref.shape[0]

    def issue(r, carry):
        for s in range(2):
            pltpu.make_async_copy(y_ref.at[pl.ds(pos_ref[0, 0, 2 * r + s], 1)],
                                  buf_ref.at[s, pl.ds(r, 1)], sem).start()
        return carry

    lax.fori_loop(0, tt, issue, 0)
    for s in range(2):
        pltpu.make_async_copy(y_ref.at[pl.ds(0, tt)], buf_ref.at[s], sem).wait()
    meta = meta_ref[...]
    o_ref[...] = (h_ref[...] + meta[:, META_GATE:META_GATE + 1] * buf_ref[0]
                  + meta[:, META_GATE + 1:META_GATE + 2] * buf_ref[1])


def _combine(pos3, y, h, meta):
    T, D = h.shape
    n_tiles, _, n_slots = pos3.shape
    tt = n_slots // 2
    return pl.pallas_call(
        _combine_kernel,
        grid=(n_tiles,),
        in_specs=[
            pl.BlockSpec((1, 1, n_slots), lambda i: (i, 0, 0), memory_space=pltpu.SMEM),
            pl.BlockSpec(memory_space=pl.ANY),
            pl.BlockSpec((tt, D), lambda i: (i, 0)),
            pl.BlockSpec((tt, LANES), lambda i: (i, 0)),
        ],
        out_specs=pl.BlockSpec((tt, D), lambda i: (i, 0)),
        out_shape=jax.ShapeDtypeStruct((T, D), F32),
        scratch_shapes=[pltpu.VMEM((2, tt, D), F32), pltpu.SemaphoreType.DMA(())],
        compiler_params=_params(("arbitrary",)),
        name="moe_combine",
    )(pos3, y, h, meta)


def _moe_routed(f, h, router_w, wg, wu, wd, *, tmg=512, tf=512, tt=256):
    T, D = h.shape
    E = wg.shape[0]
    n_tiles = 2 * T // tmg + E
    meta, counts = _router(f, router_w)
    counts = counts[0, :E].astype(jnp.int32)
    tiles_per_expert = (counts + tmg - 1) // tmg
    tile_end = jnp.cumsum(tiles_per_expert)
    row_start = (tile_end - tiles_per_expert) * tmg
    n_used = tile_end[-1:]
    tile_id = jnp.minimum(jnp.arange(n_tiles, dtype=jnp.int32), n_used - 1)
    tile_expert = jnp.sum(tile_id[:, None] >= tile_end[None, :], axis=1).astype(jnp.int32)
    expert = meta[:, META_EXPERT:META_EXPERT + 2].astype(jnp.int32)
    rank = meta[:, META_RANK:META_RANK + 2].astype(jnp.int32)
    pos3 = (row_start[expert] + rank).reshape(T // tt, 1, 2 * tt)
    xs = _dispatch(pos3, f, n_tiles * tmg)
    y = _moe_grouped_ffn(tile_expert, n_used.astype(jnp.int32), xs, wg, wu, wd, tmg=tmg, tf=tf)
    return _combine(pos3, y, h, meta)


def _alibi_slopes():
    n = HEADS_DILATED + HEADS_MOBA
    s = jnp.asarray(2.0 ** (-8.0 * np.arange(1, n + 1) / n), dtype=F32)
    return s[:HEADS_DILATED], s[HEADS_DILATED:]


def _head_norm_rows(q_gain, k_gain, D):
    a = HEADS_DILATED * HEAD_DIM
    b = HEADS_MOBA * HEAD_DIM
    rest = D - a - b

    def row(g):
        return jnp.concatenate([jnp.tile(g[0], HEADS_DILATED), jnp.tile(g[1], HEADS_MOBA), jnp.ones((rest,), F32)])

    flag = jnp.concatenate([jnp.ones((a + b,), F32), jnp.zeros((rest,), F32)])
    gain = jnp.concatenate([row(q_gain), row(k_gain), jnp.ones((D,), F32)])
    flags = jnp.concatenate([flag, flag, jnp.zeros((D,), F32)])
    return gain.reshape(1, 3 * D), flags.reshape(1, 3 * D)


def kernel(x, attn_norm, w_in, q_gain, k_gain, out_gain, w_out, ffn_norm, dense_w_gate, dense_w_up, dense_w_down, moe_router, moe_w_gate, moe_w_up, moe_w_down):
    B, S, D = x.shape
    depth = w_in.shape[0]
    slopes_a, slopes_b = _alibi_slopes()
    h = x.reshape(B * S, D)
    for layer in range(depth):
        head_gain, head_flag = _head_norm_rows(q_gain[layer], k_gain[layer], D)
        qkv = _qkv_proj(h, attn_norm[layer], w_in[layer].astype(BF16), head_gain, head_flag)
        og = out_gain[layer].reshape(1, D)
        oa = _dilated_attention(qkv, og, slopes_a, B, S)
        ob = _moba_attention(qkv, og, slopes_b, B, S)
        oc = _stick_attention(qkv, og, B, S)
        dense = layer % 2 == 0
        h, f = _out_proj(oa, ob, oc, w_out[layer].astype(BF16), h, ffn_norm[layer], BF16 if dense else F32)
        i = layer // 2
        if dense:
            h = _ffn_dense(f, h, dense_w_gate[i].astype(BF16), dense_w_up[i].astype(BF16),
                           dense_w_down[i].astype(BF16))
        else:
            h = _moe_routed(f, h, moe_router[i], moe_w_gate[i], moe_w_up[i], moe_w_down[i])
    return h.reshape(B, S, D)
```

```python
import functools

import numpy as np
import jax
import jax.numpy as jnp
from jax import lax
from jax.experimental import pallas as pl
from jax.experimental.pallas import tpu as pltpu

HEAD_DIM = 128
N_HEADS = 16
HEADS_DILATED = 6
HEADS_MOBA = 5
HEADS_STICK = 5
DILATED_PATTERNS = ((128, 1), (512, 4), (2048, 16))
MOBA_BLOCK = 256
MOBA_TOPK = 3
N_EXPERTS = 8
EPS = 1e-6

LANES = 128
VMEM_LIMIT = 56 * 1024 * 1024
MOE_VMEM_LIMIT = 60 * 1024 * 1024
ATTN_TILE = 256
NEG_INF = float("-inf")

F32 = jnp.float32
BF16 = jnp.bfloat16


def _params(sem, vmem=VMEM_LIMIT):
    return pltpu.CompilerParams(dimension_semantics=sem, vmem_limit_bytes=vmem)


def _dot(a, b):
    return jnp.dot(a, b, preferred_element_type=F32)


def _dot_nt(a, b):
    return lax.dot_general(a, b, (((1,), (1,)), ((), ())), preferred_element_type=F32)


def _rms(x, gain):
    ms = jnp.mean(x * x, axis=-1, keepdims=True)
    return x * lax.rsqrt(ms + EPS) * gain


def _split_bf16(x):
    hi = x.astype(BF16)
    lo = (x - hi.astype(F32)).astype(BF16)
    return hi, lo


def _qkv_kernel(x_ref, g_ref, w_ref, hg_ref, hf_ref, o_ref, a_ref):
    @pl.when(pl.program_id(1) == 0)
    def _():
        a_ref[...] = _rms(x_ref[...], g_ref[...]).astype(BF16)

    y = _dot(a_ref[...], w_ref[...])
    for c in range(y.shape[1] // HEAD_DIM):
        cols = slice(c * HEAD_DIM, (c + 1) * HEAD_DIM)
        blk = y[:, cols]
        normed = _rms(blk, hg_ref[:, cols])
        o_ref[:, cols] = jnp.where(hf_ref[:, cols] > 0, normed, blk).astype(o_ref.dtype)


def _qkv_proj(h, gain, w_bf16, head_gain, head_flag, *, tm=1024, tn=768):
    T, D = h.shape
    N = w_bf16.shape[1]
    return pl.pallas_call(
        _qkv_kernel,
        grid=(T // tm, N // tn),
        in_specs=[
            pl.BlockSpec((tm, D), lambda i, j: (i, 0)),
            pl.BlockSpec((1, D), lambda i, j: (0, 0)),
            pl.BlockSpec((D, tn), lambda i, j: (0, j)),
            pl.BlockSpec((1, tn), lambda i, j: (0, j)),
            pl.BlockSpec((1, tn), lambda i, j: (0, j)),
        ],
        out_specs=pl.BlockSpec((tm, tn), lambda i, j: (i, j)),
        out_shape=jax.ShapeDtypeStruct((T, N), BF16),
        scratch_shapes=[pltpu.VMEM((tm, D), BF16)],
        compiler_params=_params(("parallel", "arbitrary")),
        name="qkv_proj",
    )(h, gain.reshape(1, D), w_bf16, head_gain, head_flag)


def _attn_specs(S, n_q_tiles, head0, tq):
    q_spec = pl.BlockSpec((tq, HEAD_DIM), lambda b, h, i, *_: (b * n_q_tiles + i, head0 + h))
    k_spec = pl.BlockSpec((S, HEAD_DIM), lambda b, h, i, *_: (b, N_HEADS + head0 + h))
    v_spec = pl.BlockSpec((S, HEAD_DIM), lambda b, h, i, *_: (b, 2 * N_HEADS + head0 + h))
    g_spec = pl.BlockSpec((1, HEAD_DIM), lambda b, h, i, *_: (0, head0 + h))
    o_spec = pl.BlockSpec((tq, HEAD_DIM), lambda b, h, i, *_: (b * n_q_tiles + i, h))
    return q_spec, k_spec, v_spec, g_spec, o_spec


def _key_rows(ref, first_block, n_blocks, tq):
    return ref[pl.ds(pl.multiple_of(first_block * tq, tq), n_blocks * tq), :]


def _lane_blocks(x, tq):
    return [x[:, w * tq:(w + 1) * tq] for w in range(x.shape[1] // tq)]


def _softmax_blocks(scores, v_rows, state):
    m_blk = jnp.max(functools.reduce(jnp.maximum, scores), axis=-1, keepdims=True)
    m_new = m_blk if state is None else jnp.maximum(state[0], m_blk)
    p = [jnp.exp(s - m_new) for s in scores]
    l_blk = jnp.sum(functools.reduce(jnp.add, p), axis=-1, keepdims=True)
    pv = _dot(jnp.concatenate([x.astype(BF16) for x in p], axis=1), v_rows)
    if state is None:
        return m_new, l_blk, pv
    m, l, acc = state
    alpha = jnp.exp(m - m_new)
    return m_new, alpha * l + l_blk, alpha * acc + pv


def _dilated_kernel(q_ref, k_ref, v_ref, g_ref, bias_ref, o_ref, *, tq, n_win, scale):
    i = pl.program_id(2)
    before = jnp.minimum(i, n_win - 1)
    q = q_ref[...]
    s_all = _dot_nt(q, _key_rows(k_ref, i - before, n_win, tq))
    scores = [s * scale + bias_ref[0, before + n_win - 1 - w]
              for w, s in enumerate(_lane_blocks(s_all, tq))]
    _, l, acc = _softmax_blocks(scores, _key_rows(v_ref, i - before, n_win, tq), None)
    o_ref[...] = _rms(acc / l, g_ref[...]).astype(o_ref.dtype)


def _dilated_bias(slopes, tq, n_back, n_win):
    d = jnp.arange(n_back + n_win, dtype=jnp.int32)[:, None, None] - (n_win - 1)
    r = jnp.arange(tq, dtype=jnp.int32)[None, :, None]
    c = jnp.arange(tq, dtype=jnp.int32)[None, None, :]
    dist = d * tq + r - c
    mult = jnp.zeros(dist.shape, F32)
    for window, dilation in DILATED_PATTERNS:
        hit = (dist >= 0) & (dist <= window) & (dist % dilation == 0)
        mult = mult + hit.astype(F32)
    logm = jnp.where(mult > 0, jnp.log(jnp.maximum(mult, 1.0)), NEG_INF)
    return logm[None] - slopes[:, None, None, None] * dist.astype(F32)[None]


def _dilated_attention(qkv, out_gain, slopes, B, S, *, tq=ATTN_TILE):
    nq = S // tq
    max_window = max(w for w, _ in DILATED_PATTERNS)
    n_back = -(-max_window // tq)
    n_win = min(n_back + 1, nq)
    bias = _dilated_bias(slopes, tq, n_back, n_win)
    q_spec, k_spec, v_spec, g_spec, o_spec = _attn_specs(S, nq, 0, tq)
    bias_spec = pl.BlockSpec((1, n_back + n_win, tq, tq), lambda b, h, i: (h, 0, 0, 0))
    return pl.pallas_call(
        functools.partial(_dilated_kernel, tq=tq, n_win=n_win, scale=HEAD_DIM ** -0.5),
        grid=(B, HEADS_DILATED, nq),
        in_specs=[q_spec, k_spec, v_spec, g_spec, bias_spec],
        out_specs=o_spec,
        out_shape=jax.ShapeDtypeStruct((B * S, HEADS_DILATED * HEAD_DIM), BF16),
        compiler_params=_params(("parallel", "parallel", "arbitrary")),
        name="dilated_attn",
    )(qkv, qkv, qkv, out_gain, bias)


def _moba_kernel(slopes_ref, q_ref, k_ref, v_ref, g_ref, o_ref, kmean_ref, bias_ref,
                 *, tq, n_blocks, group, scale):
    h = pl.program_id(1)
    i = pl.program_id(2)
    slope = slopes_ref[h]

    @pl.when(i == 0)
    def _():
        kmean_ref[...] = jnp.zeros_like(kmean_ref)
        for n in range(n_blocks):
            kb = k_ref[n * tq:(n + 1) * tq, :].astype(F32)
            kmean_ref[n:n + 1, :] = jnp.mean(kb, axis=0, keepdims=True)

    q = q_ref[...]
    km_hi, km_lo = _split_bf16(kmean_ref[...])
    gate = _dot_nt(q, km_hi) + _dot_nt(q, km_lo)
    lane = lax.broadcasted_iota(jnp.int32, gate.shape, 1)
    lane_f = lane.astype(F32)
    g = jnp.where(lane < i, gate, NEG_INF)
    sel = jnp.zeros(gate.shape, jnp.bool_)
    for _ in range(MOBA_TOPK):
        top = jnp.max(g, axis=-1, keepdims=True)
        is_top = (g == top) & (top > NEG_INF)
        first = jnp.min(jnp.where(is_top, lane_f, float(LANES)), axis=-1, keepdims=True)
        pick = lane_f == first
        sel = sel | pick
        g = jnp.where(pick, NEG_INF, g)
    block_alibi = slope * ((i - lane) * tq).astype(F32)
    per_block = jnp.where(lane < i, jnp.where(sel, 0.0, NEG_INF) - block_alibi,
                          jnp.where(lane == i, 0.0, NEG_INF))

    rc = lax.broadcasted_iota(jnp.int32, (tq, tq), 0) - lax.broadcasted_iota(jnp.int32, (tq, tq), 1)
    in_block_alibi = slope * rc.astype(F32)
    own_group = i // group
    for grp in range(n_blocks // group):
        @pl.when(grp <= own_group)
        def _():
            for n in range(grp * group, (grp + 1) * group):
                b = jnp.broadcast_to(per_block[:, n:n + 1], (tq, tq)) - in_block_alibi
                bias_ref[n] = jnp.where((n == i) & (rc < 0), NEG_INF, b)

    def attend(grp, state):
        s_all = _dot_nt(q, _key_rows(k_ref, grp * group, group, tq))
        scores = [s * scale + bias_ref[grp * group + w] for w, s in enumerate(_lane_blocks(s_all, tq))]
        return _softmax_blocks(scores, _key_rows(v_ref, grp * group, group, tq), state)

    state = attend(own_group, None)
    _, l, acc = lax.fori_loop(0, own_group, attend, state)
    o_ref[...] = _rms(acc / l, g_ref[...]).astype(o_ref.dtype)


def _moba_attention(qkv, out_gain, slopes, B, S, *, tq=MOBA_BLOCK, group=4):
    nq = S // tq
    assert S % tq == 0 and nq <= LANES and nq % group == 0
    head0 = HEADS_DILATED
    q_spec, k_spec, v_spec, g_spec, o_spec = _attn_specs(S, nq, head0, tq)
    return pl.pallas_call(
        functools.partial(_moba_kernel, tq=tq, n_blocks=nq, group=group, scale=HEAD_DIM ** -0.5),
        grid=(B, HEADS_MOBA, nq),
        in_specs=[pl.BlockSpec(memory_space=pltpu.SMEM), q_spec, k_spec, v_spec, g_spec],
        out_specs=o_spec,
        scratch_shapes=[pltpu.VMEM((LANES, HEAD_DIM), F32), pltpu.VMEM((nq, tq, tq), F32)],
        out_shape=jax.ShapeDtypeStruct((B * S, HEADS_MOBA * HEAD_DIM), BF16),
        compiler_params=_params(("arbitrary", "arbitrary", "arbitrary")),
        name="moba_attn",
    )(slopes, qkv, qkv, qkv, out_gain)


def _stick_kernel(q_ref, k_ref, v_ref, g_ref, o_ref, *, kb, group, scale):
    i = pl.program_id(2)
    q = q_ref[...]
    tq = q.shape[0]
    rc = lax.broadcasted_iota(jnp.int32, (tq, kb), 0) - lax.broadcasted_iota(jnp.int32, (tq, kb), 1)
    later = (lax.broadcasted_iota(jnp.int32, (kb, kb), 0)
             > lax.broadcasted_iota(jnp.int32, (kb, kb), 1)).astype(BF16)

    def attend(grp, carry, acc, diagonal):
        z_all = _dot_nt(q, _key_rows(k_ref, grp * group, group, kb)) * scale
        z_blocks = _lane_blocks(z_all, kb)
        weights = [None] * group
        for w in reversed(range(group)):
            z = z_blocks[w]
            log_1m = jnp.minimum(-z, 0.0) - jnp.log(1.0 + jnp.exp(-jnp.abs(z)))
            if diagonal:
                past = rc > w * kb
                log_1m = jnp.where(past, log_1m, 0.0)
            hi, lo = _split_bf16(log_1m)
            after = _dot(hi, later) + _dot(lo, later) + carry
            a = jnp.exp(z + log_1m + after)
            if diagonal:
                a = jnp.where(past, a, 0.0)
            weights[w] = a.astype(BF16)
            carry = carry + jnp.sum(log_1m, axis=-1, keepdims=True)
        acc = acc + _dot(jnp.concatenate(weights, axis=1), _key_rows(v_ref, grp * group, group, kb))
        return carry, acc

    state = attend(i, jnp.zeros((tq, 1), F32), jnp.zeros((tq, HEAD_DIM), F32), True)

    def body(t, state):
        return attend(i - 1 - t, *state, False)

    _, acc = lax.fori_loop(0, i, body, state)
    o_ref[...] = _rms(acc, g_ref[...]).astype(o_ref.dtype)


def _stick_attention(qkv, out_gain, B, S, *, kb=ATTN_TILE, group=2):
    tq = kb * group
    nq = S // tq
    assert S % tq == 0
    head0 = HEADS_DILATED + HEADS_MOBA
    q_spec, k_spec, v_spec, g_spec, o_spec = _attn_specs(S, nq, head0, tq)
    return pl.pallas_call(
        functools.partial(_stick_kernel, kb=kb, group=group, scale=HEAD_DIM ** -0.5),
        grid=(B, HEADS_STICK, nq),
        in_specs=[q_spec, k_spec, v_spec, g_spec],
        out_specs=o_spec,
        out_shape=jax.ShapeDtypeStruct((B * S, HEADS_STICK * HEAD_DIM), BF16),
        compiler_params=_params(("parallel", "parallel", "arbitrary")),
        name="stick_attn",
    )(qkv, qkv, qkv, out_gain)


def _oproj_kernel(oa_ref, ob_ref, oc_ref, w_ref, h_ref, g_ref, hn_ref, f_ref):
    ka, kb = oa_ref.shape[1], ob_ref.shape[1]
    y = _dot(oa_ref[...], w_ref[0:ka, :])
    y = y + _dot(ob_ref[...], w_ref[ka:ka + kb, :])
    y = y + _dot(oc_ref[...], w_ref[ka + kb:, :])
    hn = h_ref[...] + y
    hn_ref[...] = hn
    f_ref[...] = _rms(hn, g_ref[...]).astype(f_ref.dtype)


def _out_proj(oa, ob, oc, w_bf16, h, gain, f_dtype, *, tm=512):
    T, D = h.shape
    row = lambda i: (i, 0)
    fixed = lambda i: (0, 0)
    return pl.pallas_call(
        _oproj_kernel,
        grid=(T // tm,),
        in_specs=[
            pl.BlockSpec((tm, oa.shape[1]), row),
            pl.BlockSpec((tm, ob.shape[1]), row),
            pl.BlockSpec((tm, oc.shape[1]), row),
            pl.BlockSpec((D, D), fixed),
            pl.BlockSpec((tm, D), row),
            pl.BlockSpec((1, D), fixed),
        ],
        out_specs=[pl.BlockSpec((tm, D), row), pl.BlockSpec((tm, D), row)],
        out_shape=[jax.ShapeDtypeStruct((T, D), F32), jax.ShapeDtypeStruct((T, D), f_dtype)],
        compiler_params=_params(("parallel",)),
        name="out_proj",
    )(oa, ob, oc, w_bf16, h, gain.reshape(1, D))


def _swiglu_block(f, wg_ref, wu_ref, wd_ref):
    g = _dot(f, wg_ref[...])
    u = _dot(f, wu_ref[...])
    a = (g * jax.nn.sigmoid(g) * u).astype(BF16)
    return _dot(a, wd_ref[...])


def _ffn_kernel(f_ref, wg_ref, wu_ref, wd_ref, h_ref, o_ref, acc_ref):
    j = pl.program_id(1)

    @pl.when(j == 0)
    def _():
        acc_ref[...] = h_ref[...]

    acc_ref[...] += _swiglu_block(f_ref[...], wg_ref, wu_ref, wd_ref)

    @pl.when(j == pl.num_programs(1) - 1)
    def _():
        o_ref[...] = acc_ref[...]


def _ffn_dense(f, h, wg, wu, wd, *, tm=512, tf=512):
    T, D = h.shape
    F = wg.shape[1]
    return pl.pallas_call(
        _ffn_kernel,
        grid=(T // tm, F // tf),
        in_specs=[
            pl.BlockSpec((tm, D), lambda i, j: (i, 0)),
            pl.BlockSpec((D, tf), lambda i, j: (0, j)),
            pl.BlockSpec((D, tf), lambda i, j: (0, j)),
            pl.BlockSpec((tf, D), lambda i, j: (j, 0)),
            pl.BlockSpec((tm, D), lambda i, j: (i, 0)),
        ],
        out_specs=pl.BlockSpec((tm, D), lambda i, j: (i, 0)),
        out_shape=jax.ShapeDtypeStruct((T, D), F32),
        scratch_shapes=[pltpu.VMEM((tm, D), F32)],
        compiler_params=_params(("parallel", "arbitrary")),
        name="ffn_dense",
    )(f, wg, wu, wd, h)


META_EXPERT, META_RANK, META_GATE = 0, 2, 4


def _router_kernel(f_ref, w_ref, meta_ref, count_ref):
    @pl.when(pl.program_id(0) == 0)
    def _():
        count_ref[...] = jnp.zeros_like(count_ref)

    logits = _dot(f_ref[...].astype(BF16), w_ref[...])
    tm = logits.shape[0]
    lane = lax.broadcasted_iota(jnp.int32, logits.shape, 1).astype(F32)
    g = jnp.where(lane < N_EXPERTS, logits, NEG_INF)
    picks, tops, experts = [], [], []
    for _ in range(2):
        top = jnp.max(g, axis=-1, keepdims=True)
        first = jnp.min(jnp.where(g == top, lane, float(LANES)), axis=-1, keepdims=True)
        pick = lane == first
        g = jnp.where(pick, NEG_INF, g)
        picks.append(pick)
        tops.append(top)
        experts.append(first)
    e2 = jnp.exp(tops[1] - tops[0])
    denom = 1.0 + e2
    gates = [1.0 / denom, e2 / denom]

    chosen = (picks[0] | picks[1]).astype(BF16)
    r = lax.broadcasted_iota(jnp.int32, (tm, tm), 0)
    c = lax.broadcasted_iota(jnp.int32, (tm, tm), 1)
    before = _dot((r > c).astype(BF16), chosen) + count_ref[...]
    ranks = [jnp.sum(jnp.where(p, before, 0.0), axis=-1, keepdims=True) for p in picks]
    count_ref[...] += jnp.sum(chosen.astype(F32), axis=0, keepdims=True)

    meta = jnp.zeros(logits.shape, F32)
    for base, pair in ((META_EXPERT, experts), (META_RANK, ranks), (META_GATE, gates)):
        for s in range(2):
            meta = jnp.where(lane == float(base + s), pair[s], meta)
    meta_ref[...] = meta


def _router(f, router_w, *, tm=512):
    T, D = f.shape
    w = jnp.zeros((D, LANES), BF16).at[:, :N_EXPERTS].set(router_w.astype(BF16))
    return pl.pallas_call(
        _router_kernel,
        grid=(T // tm,),
        in_specs=[pl.BlockSpec((tm, D), lambda i: (i, 0)), pl.BlockSpec((D, LANES), lambda i: (0, 0))],
        out_specs=[pl.BlockSpec((tm, LANES), lambda i: (i, 0)), pl.BlockSpec((1, LANES), lambda i: (0, 0))],
        out_shape=[jax.ShapeDtypeStruct((T, LANES), F32), jax.ShapeDtypeStruct((1, LANES), F32)],
        compiler_params=_params(("arbitrary",)),
        name="moe_router",
    )(f, w)


def _dispatch_kernel(pos_ref, f_ref, xs_init_ref, xs_ref, sem):
    del xs_init_ref
    tt = f_ref.shape[0]

    def issue(r, carry):
        src = f_ref.at[pl.ds(r, 1)]
        for s in range(2):
            pltpu.make_async_copy(src, xs_ref.at[pl.ds(pos_ref[0, 0, 2 * r + s], 1)], sem).start()
        return carry

    lax.fori_loop(0, tt, issue, 0)
    for _ in range(2):
        pltpu.make_async_copy(f_ref, xs_ref.at[pl.ds(0, tt)], sem).wait()


def _dispatch(pos3, f, n_rows):
    T, D = f.shape
    n_tiles, _, n_slots = pos3.shape
    return pl.pallas_call(
        _dispatch_kernel,
        grid=(n_tiles,),
        in_specs=[
            pl.BlockSpec((1, 1, n_slots), lambda i: (i, 0, 0), memory_space=pltpu.SMEM),
            pl.BlockSpec((n_slots // 2, D), lambda i: (i, 0)),
            pl.BlockSpec(memory_space=pl.ANY),
        ],
        out_specs=pl.BlockSpec(memory_space=pl.ANY),
        out_shape=jax.ShapeDtypeStruct((n_rows, D), f.dtype),
        scratch_shapes=[pltpu.SemaphoreType.DMA(())],
        input_output_aliases={2: 0},
        compiler_params=_params(("arbitrary",)),
        name="moe_dispatch",
    )(pos3, f, jnp.zeros((n_rows, D), f.dtype))


def _moe_group_kernel(te_ref, nu_ref, x_ref, wg_ref, wu_ref, wd_ref, y_ref, xb_ref):
    del te_ref
    g = pl.program_id(0)
    j = pl.program_id(1)
    used = g < nu_ref[0]

    @pl.when(used & (j == 0))
    def _():
        xb_ref[...] = x_ref[...].astype(BF16)

    @pl.when(used)
    def _():
        x = xb_ref[...]
        gate = _dot(x, wg_ref[0].astype(BF16))
        up = _dot(x, wu_ref[0].astype(BF16))
        a = (gate * jax.nn.sigmoid(gate) * up).astype(BF16)
        part = _dot(a, wd_ref[0].astype(BF16))

        @pl.when(j == 0)
        def _():
            y_ref[...] = part

        @pl.when(j > 0)
        def _():
            y_ref[...] += part

    @pl.when(jnp.logical_not(used) & (j == 0))
    def _():
        y_ref[...] = jnp.zeros_like(y_ref)


def _moe_grouped_ffn(tile_expert, n_used, xs, wg, wu, wd, *, tmg, tf):
    P, D = xs.shape
    E, _, F = wg.shape
    J = F // tf

    def f_block(g, j, nu):
        return jnp.where(g < nu[0], j, J - 1)

    grid_spec = pltpu.PrefetchScalarGridSpec(
        num_scalar_prefetch=2,
        grid=(P // tmg, J),
        in_specs=[
            pl.BlockSpec((tmg, D), lambda g, j, te, nu: (jnp.minimum(g, nu[0] - 1), 0)),
            pl.BlockSpec((1, D, tf), lambda g, j, te, nu: (te[g], 0, f_block(g, j, nu))),
            pl.BlockSpec((1, D, tf), lambda g, j, te, nu: (te[g], 0, f_block(g, j, nu))),
            pl.BlockSpec((1, tf, D), lambda g, j, te, nu: (te[g], f_block(g, j, nu), 0)),
        ],
        out_specs=pl.BlockSpec((tmg, D), lambda g, j, te, nu: (g, 0)),
        scratch_shapes=[pltpu.VMEM((tmg, D), BF16)],
    )
    return pl.pallas_call(
        _moe_group_kernel,
        grid_spec=grid_spec,
        out_shape=jax.ShapeDtypeStruct((P, D), F32),
        compiler_params=_params(("arbitrary", "arbitrary"), vmem=MOE_VMEM_LIMIT),
        name="moe_ffn",
    )(tile_expert, n_used, xs, wg, wu, wd)


def _combine_kernel(pos_ref, y_ref, h_ref, meta_ref, o_ref, buf_ref, sem):
    tt = h_ref.shape[0]

    def issue(r, carry):
        for s in range(2):
            pltpu.make_async_copy(y_ref.at[pl.ds(pos_ref[0, 0, 2 * r + s], 1)],
                                  buf_ref.at[s, pl.ds(r, 1)], sem).start()
        return carry

    lax.fori_loop(0, tt, issue, 0)
    for s in range(2):
        pltpu.make_async_copy(y_ref.at[pl.ds(0, tt)], buf_ref.at[s], sem).wait()
    meta = meta_ref[...]
    o_ref[...] = (h_ref[...] + meta[:, META_GATE:META_GATE + 1] * buf_ref[0]
                  + meta[:, META_GATE + 1:META_GATE + 2] * buf_ref[1])


def _combine(pos3, y, h, meta):
    T, D = h.shape
    n_tiles, _, n_slots = pos3.shape
    tt = n_slots // 2
    return pl.pallas_call(
        _combine_kernel,
        grid=(n_tiles,),
        in_specs=[
            pl.BlockSpec((1, 1, n_slots), lambda i: (i, 0, 0), memory_space=pltpu.SMEM),
            pl.BlockSpec(memory_space=pl.ANY),
            pl.BlockSpec((tt, D), lambda i: (i, 0)),
            pl.BlockSpec((tt, LANES), lambda i: (i, 0)),
        ],
        out_specs=pl.BlockSpec((tt, D), lambda i: (i, 0)),
        out_shape=jax.ShapeDtypeStruct((T, D), F32),
        scratch_shapes=[pltpu.VMEM((2, tt, D), F32), pltpu.SemaphoreType.DMA(())],
        compiler_params=_params(("arbitrary",)),
        name="moe_combine",
    )(pos3, y, h, meta)


def _moe_routed(f, h, router_w, wg, wu, wd, *, tmg=512, tf=512, tt=256):
    T, D = h.shape
    E = wg.shape[0]
    n_tiles = 2 * T // tmg + E
    meta, counts = _router(f, router_w)
    counts = counts[0, :E].astype(jnp.int32)
    tiles_per_expert = (counts + tmg - 1) // tmg
    tile_end = jnp.cumsum(tiles_per_expert)
    row_start = (tile_end - tiles_per_expert) * tmg
    n_used = tile_end[-1:]
    tile_id = jnp.minimum(jnp.arange(n_tiles, dtype=jnp.int32), n_used - 1)
    tile_expert = jnp.sum(tile_id[:, None] >= tile_end[None, :], axis=1).astype(jnp.int32)
    expert = meta[:, META_EXPERT:META_EXPERT + 2].astype(jnp.int32)
    rank = meta[:, META_RANK:META_RANK + 2].astype(jnp.int32)
    pos3 = (row_start[expert] + rank).reshape(T // tt, 1, 2 * tt)
    xs = _dispatch(pos3, f, n_tiles * tmg)
    y = _moe_grouped_ffn(tile_expert, n_used.astype(jnp.int32), xs, wg, wu, wd, tmg=tmg, tf=tf)
    return _combine(pos3, y, h, meta)


def _alibi_slopes():
    n = HEADS_DILATED + HEADS_MOBA
    s = jnp.asarray(2.0 ** (-8.0 * np.arange(1, n + 1) / n), dtype=F32)
    return s[:HEADS_DILATED], s[HEADS_DILATED:]


def _head_norm_rows(q_gain, k_gain, D):
    a = HEADS_DILATED * HEAD_DIM
    b = HEADS_MOBA * HEAD_DIM
    rest = D - a - b

    def row(g):
        return jnp.concatenate([jnp.tile(g[0], HEADS_DILATED), jnp.tile(g[1], HEADS_MOBA), jnp.ones((rest,), F32)])

    flag = jnp.concatenate([jnp.ones((a + b,), F32), jnp.zeros((rest,), F32)])
    gain = jnp.concatenate([row(q_gain), row(k_gain), jnp.ones((D,), F32)])
    flags = jnp.concatenate([flag, flag, jnp.zeros((D,), F32)])
    return gain.reshape(1, 3 * D), flags.reshape(1, 3 * D)


def kernel(x, attn_norm, w_in, q_gain, k_gain, out_gain, w_out, ffn_norm, dense_w_gate, dense_w_up, dense_w_down, moe_router, moe_w_gate, moe_w_up, moe_w_down):
    B, S, D = x.shape
    depth = w_in.shape[0]
    slopes_a, slopes_b = _alibi_slopes()
    h = x.reshape(B * S, D)
    for layer in range(depth):
        head_gain, head_flag = _head_norm_rows(q_gain[layer], k_gain[layer], D)
        qkv = _qkv_proj(h, attn_norm[layer], w_in[layer].astype(BF16), head_gain, head_flag)
        og = out_gain[layer].reshape(1, D)
        oa = _dilated_attention(qkv, og, slopes_a, B, S)
        ob = _moba_attention(qkv, og, slopes_b, B, S)
        oc = _stick_attention(qkv, og, B, S)
        dense = layer % 2 == 0
        h, f = _out_proj(oa, ob, oc, w_out[layer].astype(BF16), h, ffn_norm[layer], BF16 if dense else F32)
        i = layer // 2
        if dense:
            h = _ffn_dense(f, h, dense_w_gate[i].astype(BF16), dense_w_up[i].astype(BF16),
                           dense_w_down[i].astype(BF16))
        else:
            h = _moe_routed(f, h, moe_router[i], moe_w_gate[i], moe_w_up[i], moe_w_down[i])
    return h.reshape(B, S, D)
```

```python
import functools

import numpy as np
import jax
import jax.numpy as jnp
from jax import lax
from jax.experimental import pallas as pl
from jax.experimental.pallas import tpu as pltpu

HEAD_DIM = 128
N_HEADS = 16
HEADS_DILATED = 6
HEADS_MOBA = 5
HEADS_STICK = 5
DILATED_PATTERNS = ((128, 1), (512, 4), (2048, 16))
MOBA_BLOCK = 256
MOBA_TOPK = 3
N_EXPERTS = 8
EPS = 1e-6

LANES = 128
VMEM_LIMIT = 56 * 1024 * 1024
MOE_VMEM_LIMIT = 60 * 1024 * 1024
ATTN_TILE = 256
NEG_INF = float("-inf")
LOG2E = 1.4426950408889634
Q_PRESCALE = HEAD_DIM ** -0.5 * LOG2E

F32 = jnp.float32
BF16 = jnp.bfloat16


def _params(sem, vmem=VMEM_LIMIT):
    return pltpu.CompilerParams(dimension_semantics=sem, vmem_limit_bytes=vmem)


def _dot(a, b):
    return jnp.dot(a, b, preferred_element_type=F32)


def _dot_nt(a, b):
    return lax.dot_general(a, b, (((1,), (1,)), ((), ())), preferred_element_type=F32)


def _rms(x, gain):
    ms = jnp.mean(x * x, axis=-1, keepdims=True)
    return x * lax.rsqrt(ms + EPS) * gain


def _split_bf16(x):
    hi = x.astype(BF16)
    lo = (x - hi.astype(F32)).astype(BF16)
    return hi, lo


def _qkv_kernel(x_ref, g_ref, w_ref, hg_ref, hf_ref, o_ref, a_ref):
    @pl.when(pl.program_id(1) == 0)
    def _():
        a_ref[...] = _rms(x_ref[...], g_ref[...]).astype(BF16)

    y = _dot(a_ref[...], w_ref[...])
    for c in range(y.shape[1] // HEAD_DIM):
        cols = slice(c * HEAD_DIM, (c + 1) * HEAD_DIM)
        blk = y[:, cols]
        ms = jnp.mean(blk * blk, axis=-1, keepdims=True)
        normed = blk * lax.rsqrt(ms + EPS)
        o_ref[:, cols] = (jnp.where(hf_ref[:, cols] > 0, normed, blk) * hg_ref[:, cols]).astype(o_ref.dtype)


def _qkv_proj(h, gain, w_bf16, head_gain, head_flag, *, tm=1024, tn=768):
    T, D = h.shape
    N = w_bf16.shape[1]
    return pl.pallas_call(
        _qkv_kernel,
        grid=(T // tm, N // tn),
        in_specs=[
            pl.BlockSpec((tm, D), lambda i, j: (i, 0)),
            pl.BlockSpec((1, D), lambda i, j: (0, 0)),
            pl.BlockSpec((D, tn), lambda i, j: (0, j)),
            pl.BlockSpec((1, tn), lambda i, j: (0, j)),
            pl.BlockSpec((1, tn), lambda i, j: (0, j)),
        ],
        out_specs=pl.BlockSpec((tm, tn), lambda i, j: (i, j)),
        out_shape=jax.ShapeDtypeStruct((T, N), BF16),
        scratch_shapes=[pltpu.VMEM((tm, D), BF16)],
        compiler_params=_params(("parallel", "arbitrary")),
        name="qkv_proj",
    )(h, gain.reshape(1, D), w_bf16, head_gain, head_flag)


def _attn_specs(S, n_q_tiles, head0, tq):
    q_spec = pl.BlockSpec((tq, HEAD_DIM), lambda b, h, i, *_: (b * n_q_tiles + i, head0 + h))
    k_spec = pl.BlockSpec((S, HEAD_DIM), lambda b, h, i, *_: (b, N_HEADS + head0 + h))
    v_spec = pl.BlockSpec((S, HEAD_DIM), lambda b, h, i, *_: (b, 2 * N_HEADS + head0 + h))
    g_spec = pl.BlockSpec((1, HEAD_DIM), lambda b, h, i, *_: (0, head0 + h))
    o_spec = pl.BlockSpec((tq, HEAD_DIM), lambda b, h, i, *_: (b * n_q_tiles + i, h))
    return q_spec, k_spec, v_spec, g_spec, o_spec


def _key_rows(ref, first_block, n_blocks, tq):
    return ref[pl.ds(pl.multiple_of(first_block * tq, tq), n_blocks * tq), :]


def _lane_blocks(x, tq):
    return [x[:, w * tq:(w + 1) * tq] for w in range(x.shape[1] // tq)]


def _partial_softmax(scores, v_rows):
    m = jnp.max(functools.reduce(jnp.maximum, scores), axis=-1, keepdims=True)
    m_safe = jnp.where(m == NEG_INF, 0.0, m)
    p = [jnp.exp2(s - m_safe) for s in scores]
    l = jnp.sum(functools.reduce(jnp.add, p), axis=-1, keepdims=True)
    return m, l, _dot(jnp.concatenate([x.astype(BF16) for x in p], axis=1), v_rows)


def _merged_softmax(parts):
    m = functools.reduce(jnp.maximum, [part[0] for part in parts])
    weights = [jnp.exp2(part[0] - m) for part in parts]
    l = functools.reduce(jnp.add, [w * part[1] for w, part in zip(weights, parts)])
    acc = functools.reduce(jnp.add, [w * part[2] for w, part in zip(weights, parts)])
    return acc / l


def _dilated_kernel(q_ref, k_ref, v_ref, g_ref, bias_ref, o_ref, *, tq, n_win, split):
    i = pl.program_id(2)
    before = jnp.minimum(i, n_win - 1)
    q = q_ref[...]
    parts = []
    for w0 in range(0, n_win, split):
        n = min(split, n_win - w0)
        s_all = _dot_nt(q, _key_rows(k_ref, i - before + w0, n, tq))
        scores = [s + bias_ref[0, before + n_win - 1 - (w0 + w)]
                  for w, s in enumerate(_lane_blocks(s_all, tq))]
        parts.append(_partial_softmax(scores, _key_rows(v_ref, i - before + w0, n, tq)))
    o_ref[...] = _rms(_merged_softmax(parts), g_ref[...]).astype(o_ref.dtype)


def _dilated_bias(slopes, tq, n_back, n_win):
    d = jnp.arange(n_back + n_win, dtype=jnp.int32)[:, None, None] - (n_win - 1)
    r = jnp.arange(tq, dtype=jnp.int32)[None, :, None]
    c = jnp.arange(tq, dtype=jnp.int32)[None, None, :]
    dist = d * tq + r - c
    mult = jnp.zeros(dist.shape, F32)
    for window, dilation in DILATED_PATTERNS:
        hit = (dist >= 0) & (dist <= window) & (dist % dilation == 0)
        mult = mult + hit.astype(F32)
    logm = jnp.where(mult > 0, jnp.log2(jnp.maximum(mult, 1.0)), NEG_INF)
    return logm[None] - (slopes * LOG2E)[:, None, None, None] * dist.astype(F32)[None]


def _dilated_attention(qkv, out_gain, slopes, B, S, *, tq=ATTN_TILE):
    nq = S // tq
    max_window = max(w for w, _ in DILATED_PATTERNS)
    n_back = -(-max_window // tq)
    n_win = min(n_back + 1, nq)
    bias = _dilated_bias(slopes, tq, n_back, n_win)
    q_spec, k_spec, v_spec, g_spec, o_spec = _attn_specs(S, nq, 0, tq)
    bias_spec = pl.BlockSpec((1, n_back + n_win, tq, tq), lambda b, h, i: (h, 0, 0, 0))
    return pl.pallas_call(
        functools.partial(_dilated_kernel, tq=tq, n_win=n_win, split=n_win),
        grid=(B, HEADS_DILATED, nq),
        in_specs=[q_spec, k_spec, v_spec, g_spec, bias_spec],
        out_specs=o_spec,
        out_shape=jax.ShapeDtypeStruct((B * S, HEADS_DILATED * HEAD_DIM), BF16),
        compiler_params=_params(("parallel", "parallel", "arbitrary")),
        name="dilated_attn",
    )(qkv, qkv, qkv, out_gain, bias)


def _moba_kernel(slopes_ref, q_ref, k_ref, v_ref, g_ref, o_ref, kmean_ref, *, kb, n_blocks, group):
    h = pl.program_id(1)
    i = pl.program_id(2)
    slope = slopes_ref[h]

    @pl.when(i == 0)
    def _():
        kmean_ref[...] = jnp.zeros_like(kmean_ref)
        for n in range(n_blocks):
            rows = k_ref[n * kb:(n + 1) * kb, :].astype(F32)
            kmean_ref[n:n + 1, :] = jnp.mean(rows, axis=0, keepdims=True)

    q = q_ref[...]
    tq = q.shape[0]
    km_hi, km_lo = _split_bf16(kmean_ref[...])
    gate = _dot_nt(q, km_hi) + _dot_nt(q, km_lo)
    lane = lax.broadcasted_iota(jnp.int32, gate.shape, 1)
    lane_f = lane.astype(F32)
    row = lax.broadcasted_iota(jnp.int32, (tq, 1), 0)
    own = i * (tq // kb) + row // kb
    g = jnp.where(lane < own, gate, NEG_INF)
    sel = jnp.zeros(gate.shape, jnp.bool_)
    for _ in range(MOBA_TOPK):
        top = jnp.max(g, axis=-1, keepdims=True)
        is_top = (g == top) & (top > NEG_INF)
        first = jnp.min(jnp.where(is_top, lane_f, float(LANES)), axis=-1, keepdims=True)
        pick = lane_f == first
        sel = sel | pick
        g = jnp.where(pick, NEG_INF, g)
    block_alibi = slope * ((own - lane) * kb).astype(F32)
    per_block = jnp.where(lane < own, jnp.where(sel, 0.0, NEG_INF) - block_alibi,
                          jnp.where(lane == own, 0.0, NEG_INF))

    rc = (lax.broadcasted_iota(jnp.int32, (tq, kb), 0) % kb) - lax.broadcasted_iota(jnp.int32, (tq, kb), 1)
    in_block_alibi = slope * rc.astype(F32)
    own_group = (i * (tq // kb)) // group

    def attend(grp):
        rows = slice(grp * group * kb, (grp + 1) * group * kb)
        scores = []
        for w, s in enumerate(_lane_blocks(_dot_nt(q, k_ref[rows, :]), kb)):
            n = grp * group + w
            bias = jnp.broadcast_to(per_block[:, n:n + 1], (tq, kb)) - in_block_alibi
            scores.append(s + jnp.where((own == n) & (rc < 0), NEG_INF, bias))
        return _partial_softmax(scores, v_ref[rows, :])

    for last in range(n_blocks // group):
        @pl.when(own_group == last)
        def _():
            out = _merged_softmax([attend(grp) for grp in range(last + 1)])
            o_ref[...] = _rms(out, g_ref[...]).astype(o_ref.dtype)


def _moba_attention(qkv, out_gain, slopes, B, S, *, kb=MOBA_BLOCK, tq=MOBA_BLOCK, group=4):
    n_blocks = S // kb
    nq = S // tq
    assert S % tq == 0 and n_blocks <= LANES and n_blocks % group == 0 and (group * kb) % tq == 0
    head0 = HEADS_DILATED
    q_spec, k_spec, v_spec, g_spec, o_spec = _attn_specs(S, nq, head0, tq)
    return pl.pallas_call(
        functools.partial(_moba_kernel, kb=kb, n_blocks=n_blocks, group=group),
        grid=(B, HEADS_MOBA, nq),
        in_specs=[pl.BlockSpec(memory_space=pltpu.SMEM), q_spec, k_spec, v_spec, g_spec],
        out_specs=o_spec,
        scratch_shapes=[pltpu.VMEM((LANES, HEAD_DIM), F32)],
        out_shape=jax.ShapeDtypeStruct((B * S, HEADS_MOBA * HEAD_DIM), BF16),
        compiler_params=_params(("arbitrary", "arbitrary", "arbitrary")),
        name="moba_attn",
    )(slopes * LOG2E, qkv, qkv, qkv, out_gain)


def _stick_kernel(q_ref, k_ref, v_ref, g_ref, o_ref, *, kb, group):
    i = pl.program_id(2)
    q = q_ref[...]
    tq = q.shape[0]
    rc = lax.broadcasted_iota(jnp.int32, (tq, kb), 0) - lax.broadcasted_iota(jnp.int32, (tq, kb), 1)
    later = (lax.broadcasted_iota(jnp.int32, (kb, kb), 0)
             > lax.broadcasted_iota(jnp.int32, (kb, kb), 1)).astype(BF16)

    def attend(grp, carry, diagonal):
        z_blocks = _lane_blocks(_dot_nt(q, _key_rows(k_ref, grp * group, group, kb)), kb)
        weights = [None] * group
        for w in reversed(range(group)):
            z = z_blocks[w]
            log_1m = jnp.minimum(-z, 0.0) - jnp.log2(1.0 + jnp.exp2(-jnp.abs(z)))
            if diagonal:
                past = rc > w * kb
                log_1m = jnp.where(past, log_1m, 0.0)
            hi, lo = _split_bf16(log_1m)
            after = _dot(hi, later) + _dot(lo, later) + carry
            a = jnp.exp2(z + log_1m + after)
            if diagonal:
                a = jnp.where(past, a, 0.0)
            weights[w] = a.astype(BF16)
            carry = carry + jnp.sum(log_1m, axis=-1, keepdims=True)
        return carry, _dot(jnp.concatenate(weights, axis=1), _key_rows(v_ref, grp * group, group, kb))

    carry, acc = attend(i, jnp.zeros((tq, 1), F32), True)

    odd = i % 2

    def single(state):
        carry, out = attend(i - 1, state[0], False)
        return carry, state[1] + out

    carry, acc = lax.cond(odd == 1, single, lambda state: state, (carry, acc))

    def body(t, state):
        first = i - 1 - odd - 2 * t
        carry, out_a = attend(first, state[0], False)
        carry, out_b = attend(first - 1, carry, False)
        return carry, state[1] + out_a + out_b

    _, acc = lax.fori_loop(0, i // 2, body, (carry, acc))
    o_ref[...] = _rms(acc, g_ref[...]).astype(o_ref.dtype)


def _stick_attention(qkv, out_gain, B, S, *, kb=ATTN_TILE, group=2):
    tq = kb * group
    nq = S // tq
    assert S % tq == 0
    head0 = HEADS_DILATED + HEADS_MOBA
    q_spec, k_spec, v_spec, g_spec, o_spec = _attn_specs(S, nq, head0, tq)
    return pl.pallas_call(
        functools.partial(_stick_kernel, kb=kb, group=group),
        grid=(B, HEADS_STICK, nq),
        in_specs=[q_spec, k_spec, v_spec, g_spec],
        out_specs=o_spec,
        out_shape=jax.ShapeDtypeStruct((B * S, HEADS_STICK * HEAD_DIM), BF16),
        compiler_params=_params(("parallel", "parallel", "arbitrary")),
        name="stick_attn",
    )(qkv, qkv, qkv, out_gain)


def _oproj_kernel(oa_ref, ob_ref, oc_ref, w_ref, h_ref, g_ref, hn_ref, f_ref):
    ka, kb = oa_ref.shape[1], ob_ref.shape[1]
    y = _dot(oa_ref[...], w_ref[0:ka, :])
    y = y + _dot(ob_ref[...], w_ref[ka:ka + kb, :])
    y = y + _dot(oc_ref[...], w_ref[ka + kb:, :])
    hn = h_ref[...] + y
    hn_ref[...] = hn
    f_ref[...] = _rms(hn, g_ref[...]).astype(f_ref.dtype)


def _out_proj(oa, ob, oc, w_bf16, h, gain, f_dtype, *, tm=512):
    T, D = h.shape
    row = lambda i: (i, 0)
    fixed = lambda i: (0, 0)
    return pl.pallas_call(
        _oproj_kernel,
        grid=(T // tm,),
        in_specs=[
            pl.BlockSpec((tm, oa.shape[1]), row),
            pl.BlockSpec((tm, ob.shape[1]), row),
            pl.BlockSpec((tm, oc.shape[1]), row),
            pl.BlockSpec((D, D), fixed),
            pl.BlockSpec((tm, D), row),
            pl.BlockSpec((1, D), fixed),
        ],
        out_specs=[pl.BlockSpec((tm, D), row), pl.BlockSpec((tm, D), row)],
        out_shape=[jax.ShapeDtypeStruct((T, D), F32), jax.ShapeDtypeStruct((T, D), f_dtype)],
        compiler_params=_params(("parallel",)),
        name="out_proj",
    )(oa, ob, oc, w_bf16, h, gain.reshape(1, D))


def _swiglu_block(f, wg_ref, wu_ref, wd_ref):
    g = _dot(f, wg_ref[...])
    u = _dot(f, wu_ref[...])
    a = (g * jax.nn.sigmoid(g) * u).astype(BF16)
    return _dot(a, wd_ref[...])


def _ffn_kernel(f_ref, wg_ref, wu_ref, wd_ref, h_ref, o_ref, acc_ref):
    j = pl.program_id(1)

    @pl.when(j == 0)
    def _():
        acc_ref[...] = h_ref[...]

    acc_ref[...] += _swiglu_block(f_ref[...], wg_ref, wu_ref, wd_ref)

    @pl.when(j == pl.num_programs(1) - 1)
    def _():
        o_ref[...] = acc_ref[...]


def _ffn_dense(f, h, wg, wu, wd, *, tm=512, tf=512):
    T, D = h.shape
    F = wg.shape[1]
    return pl.pallas_call(
        _ffn_kernel,
        grid=(T // tm, F // tf),
        in_specs=[
            pl.BlockSpec((tm, D), lambda i, j: (i, 0)),
            pl.BlockSpec((D, tf), lambda i, j: (0, j)),
            pl.BlockSpec((D, tf), lambda i, j: (0, j)),
            pl.BlockSpec((tf, D), lambda i, j: (j, 0)),
            pl.BlockSpec((tm, D), lambda i, j: (i, 0)),
        ],
        out_specs=pl.BlockSpec((tm, D), lambda i, j: (i, 0)),
        out_shape=jax.ShapeDtypeStruct((T, D), F32),
        scratch_shapes=[pltpu.VMEM((tm, D), F32)],
        compiler_params=_params(("parallel", "arbitrary")),
        name="ffn_dense",
    )(f, wg, wu, wd, h)


META_EXPERT, META_RANK, META_GATE = 0, 2, 4


def _router_kernel(f_ref, w_ref, meta_ref, count_ref):
    @pl.when(pl.program_id(0) == 0)
    def _():
        count_ref[...] = jnp.zeros_like(count_ref)

    logits = _dot(f_ref[...].astype(BF16), w_ref[...])
    tm = logits.shape[0]
    lane = lax.broadcasted_iota(jnp.int32, logits.shape, 1).astype(F32)
    g = jnp.where(lane < N_EXPERTS, logits, NEG_INF)
    picks, tops, experts = [], [], []
    for _ in range(2):
        top = jnp.max(g, axis=-1, keepdims=True)
        first = jnp.min(jnp.where(g == top, lane, float(LANES)), axis=-1, keepdims=True)
        pick = lane == first
        g = jnp.where(pick, NEG_INF, g)
        picks.append(pick)
        tops.append(top)
        experts.append(first)
    e2 = jnp.exp(tops[1] - tops[0])
    denom = 1.0 + e2
    gates = [1.0 / denom, e2 / denom]

    chosen = (picks[0] | picks[1]).astype(BF16)
    r = lax.broadcasted_iota(jnp.int32, (tm, tm), 0)
    c = lax.broadcasted_iota(jnp.int32, (tm, tm), 1)
    before = _dot((r > c).astype(BF16), chosen) + count_ref[...]
    ranks = [jnp.sum(jnp.where(p, before, 0.0), axis=-1, keepdims=True) for p in picks]
    count_ref[...] += jnp.sum(chosen.astype(F32), axis=0, keepdims=True)

    meta = jnp.zeros(logits.shape, F32)
    for base, pair in ((META_EXPERT, experts), (META_RANK, ranks), (META_GATE, gates)):
        for s in range(2):
            meta = jnp.where(lane == float(base + s), pair[s], meta)
    meta_ref[...] = meta


def _router(f, router_w, *, tm=512):
    T, D = f.shape
    w = jnp.zeros((D, LANES), BF16).at[:, :N_EXPERTS].set(router_w.astype(BF16))
    return pl.pallas_call(
        _router_kernel,
        grid=(T // tm,),
        in_specs=[pl.BlockSpec((tm, D), lambda i: (i, 0)), pl.BlockSpec((D, LANES), lambda i: (0, 0))],
        out_specs=[pl.BlockSpec((tm, LANES), lambda i: (i, 0)), pl.BlockSpec((1, LANES), lambda i: (0, 0))],
        out_shape=[jax.ShapeDtypeStruct((T, LANES), F32), jax.ShapeDtypeStruct((1, LANES), F32)],
        compiler_params=_params(("arbitrary",)),
        name="moe_router",
    )(f, w)


def _dispatch_kernel(pos_ref, f_ref, xs_init_ref, xs_ref, sem):
    del xs_init_ref
    tt = f_ref.shape[0]

    def issue(r, carry):
        src = f_ref.at[pl.ds(r, 1)]
        for s in range(2):
            pltpu.make_async_copy(src, xs_ref.at[pl.ds(pos_ref[0, 0, 2 * r + s], 1)], sem).start()
        return carry

    lax.fori_loop(0, tt, issue, 0)
    for _ in range(2):
        pltpu.make_async_copy(f_ref, xs_ref.at[pl.ds(0, tt)], sem).wait()


def _dispatch(pos3, f, n_rows):
    T, D = f.shape
    n_tiles, _, n_slots = pos3.shape
    return pl.pallas_call(
        _dispatch_kernel,
        grid=(n_tiles,),
        in_specs=[
            pl.BlockSpec((1, 1, n_slots), lambda i: (i, 0, 0), memory_space=pltpu.SMEM),
            pl.BlockSpec((n_slots // 2, D), lambda i: (i, 0)),
            pl.BlockSpec(memory_space=pl.ANY),
        ],
        out_specs=pl.BlockSpec(memory_space=pl.ANY),
        out_shape=jax.ShapeDtypeStruct((n_rows, D), f.dtype),
        scratch_shapes=[pltpu.SemaphoreType.DMA(())],
        input_output_aliases={2: 0},
        compiler_params=_params(("arbitrary",)),
        name="moe_dispatch",
    )(pos3, f, jnp.zeros((n_rows, D), f.dtype))


def _moe_group_kernel(te_ref, nu_ref, x_ref, wg_ref, wu_ref, wd_ref, y_ref, xb_ref):
    del te_ref
    g = pl.program_id(0)
    j = pl.program_id(1)
    used = g < nu_ref[0]

    @pl.when(used & (j == 0))
    def _():
        xb_ref[...] = x_ref[...].astype(BF16)

    @pl.when(used)
    def _():
        x = xb_ref[...]
        gate = _dot(x, wg_ref[0].astype(BF16))
        up = _dot(x, wu_ref[0].astype(BF16))
        a = (gate * jax.nn.sigmoid(gate) * up).astype(BF16)
        part = _dot(a, wd_ref[0].astype(BF16))

        @pl.when(j == 0)
        def _():
            y_ref[...] = part

        @pl.when(j > 0)
        def _():
            y_ref[...] += part

    @pl.when(jnp.logical_not(used) & (j == 0))
    def _():
        y_ref[...] = jnp.zeros_like(y_ref)


def _moe_grouped_ffn(tile_expert, n_used, xs, wg, wu, wd, *, tmg, tf):
    P, D = xs.shape
    E, _, F = wg.shape
    J = F // tf

    def f_block(g, j, nu):
        return jnp.where(g < nu[0], j, J - 1)

    grid_spec = pltpu.PrefetchScalarGridSpec(
        num_scalar_prefetch=2,
        grid=(P // tmg, J),
        in_specs=[
            pl.BlockSpec((tmg, D), lambda g, j, te, nu: (jnp.minimum(g, nu[0] - 1), 0)),
            pl.BlockSpec((1, D, tf), lambda g, j, te, nu: (te[g], 0, f_block(g, j, nu))),
            pl.BlockSpec((1, D, tf), lambda g, j, te, nu: (te[g], 0, f_block(g, j, nu))),
            pl.BlockSpec((1, tf, D), lambda g, j, te, nu: (te[g], f_block(g, j, nu), 0)),
        ],
        out_specs=pl.BlockSpec((tmg, D), lambda g, j, te, nu: (g, 0)),
        scratch_shapes=[pltpu.VMEM((tmg, D), BF16)],
    )
    return pl.pallas_call(
        _moe_group_kernel,
        grid_spec=grid_spec,
        out_shape=jax.ShapeDtypeStruct((P, D), F32),
        compiler_params=_params(("arbitrary", "arbitrary"), vmem=MOE_VMEM_LIMIT),
        name="moe_ffn",
    )(tile_expert, n_used, xs, wg, wu, wd)


def _combine_kernel(pos_ref, y_ref, h_ref, meta_ref, o_ref, buf_ref, sem):
    tt = h_ref.shape[0]

    def issue(r, carry):
        for s in range(2):
            pltpu.make_async_copy(y_ref.at[pl.ds(pos_ref[0, 0, 2 * r + s], 1)],
                                  buf_ref.at[s, pl.ds(r, 1)], sem).start()
        return carry

    lax.fori_loop(0, tt, issue, 0)
    for s in range(2):
        pltpu.make_async_copy(y_ref.at[pl.ds(0, tt)], buf_ref.at[s], sem).wait()
    meta = meta_ref[...]
    o_ref[...] = (h_ref[...] + meta[:, META_GATE:META_GATE + 1] * buf_ref[0]
                  + meta[:, META_GATE + 1:META_GATE + 2] * buf_ref[1])


def _combine(pos3, y, h, meta):
    T, D = h.shape
    n_tiles, _, n_slots = pos3.shape
    tt = n_slots // 2
    return pl.pallas_call(
        _combine_kernel,
        grid=(n_tiles,),
        in_specs=[
            pl.BlockSpec((1, 1, n_slots), lambda i: (i, 0, 0), memory_space=pltpu.SMEM),
            pl.BlockSpec(memory_space=pl.ANY),
            pl.BlockSpec((tt, D), lambda i: (i, 0)),
            pl.BlockSpec((tt, LANES), lambda i: (i, 0)),
        ],
        out_specs=pl.BlockSpec((tt, D), lambda i: (i, 0)),
        out_shape=jax.ShapeDtypeStruct((T, D), F32),
        scratch_shapes=[pltpu.VMEM((2, tt, D), F32), pltpu.SemaphoreType.DMA(())],
        compiler_params=_params(("arbitrary",)),
        name="moe_combine",
    )(pos3, y, h, meta)


def _moe_routed(f, h, router_w, wg, wu, wd, *, tmg=768, tf=512, tt=256):
    T, D = h.shape
    E = wg.shape[0]
    n_tiles = 2 * T // tmg + E
    meta, counts = _router(f, router_w)
    counts = counts[0, :E].astype(jnp.int32)
    tiles_per_expert = (counts + tmg - 1) // tmg
    tile_end = jnp.cumsum(tiles_per_expert)
    row_start = (tile_end - tiles_per_expert) * tmg
    n_used = tile_end[-1:]
    tile_id = jnp.minimum(jnp.arange(n_tiles, dtype=jnp.int32), n_used - 1)
    tile_expert = jnp.sum(tile_id[:, None] >= tile_end[None, :], axis=1).astype(jnp.int32)
    expert = meta[:, META_EXPERT:META_EXPERT + 2].astype(jnp.int32)
    rank = meta[:, META_RANK:META_RANK + 2].astype(jnp.int32)
    pos3 = (row_start[expert] + rank).reshape(T // tt, 1, 2 * tt)
    xs = _dispatch(pos3, f, n_tiles * tmg)
    y = _moe_grouped_ffn(tile_expert, n_used.astype(jnp.int32), xs, wg, wu, wd, tmg=tmg, tf=tf)
    return _combine(pos3, y, h, meta)


def _alibi_slopes():
    n = HEADS_DILATED + HEADS_MOBA
    s = jnp.asarray(2.0 ** (-8.0 * np.arange(1, n + 1) / n), dtype=F32)
    return s[:HEADS_DILATED], s[HEADS_DILATED:]


def _head_norm_rows(q_gain, k_gain, D):
    a = HEADS_DILATED * HEAD_DIM
    b = HEADS_MOBA * HEAD_DIM
    rest = D - a - b

    def row(g):
        return jnp.concatenate([jnp.tile(g[0], HEADS_DILATED), jnp.tile(g[1], HEADS_MOBA), jnp.ones((rest,), F32)])

    flag = jnp.concatenate([jnp.ones((a + b,), F32), jnp.zeros((rest,), F32)])
    gain = jnp.concatenate([row(q_gain) * Q_PRESCALE, row(k_gain), jnp.ones((D,), F32)])
    flags = jnp.concatenate([flag, flag, jnp.zeros((D,), F32)])
    return gain.reshape(1, 3 * D), flags.reshape(1, 3 * D)


def kernel(x, attn_norm, w_in, q_gain, k_gain, out_gain, w_out, ffn_norm, dense_w_gate, dense_w_up, dense_w_down, moe_router, moe_w_gate, moe_w_up, moe_w_down):
    B, S, D = x.shape
    depth = w_in.shape[0]
    slopes_a, slopes_b = _alibi_slopes()
    h = x.reshape(B * S, D)
    for layer in range(depth):
        head_gain, head_flag = _head_norm_rows(q_gain[layer], k_gain[layer], D)
        qkv = _qkv_proj(h, attn_norm[layer], w_in[layer].astype(BF16), head_gain, head_flag)
        og = out_gain[layer].reshape(1, D)
        oa = _dilated_attention(qkv, og, slopes_a, B, S)
        ob = _moba_attention(qkv, og, slopes_b, B, S)
        oc = _stick_attention(qkv, og, B, S)
        dense = layer % 2 == 0
        h, f = _out_proj(oa, ob, oc, w_out[layer].astype(BF16), h, ffn_norm[layer], BF16 if dense else F32)
        i = layer // 2
        if dense:
            h = _ffn_dense(f, h, dense_w_gate[i].astype(BF16), dense_w_up[i].astype(BF16),
                           dense_w_down[i].astype(BF16))
        else:
            h = _moe_routed(f, h, moe_router[i], moe_w_gate[i], moe_w_up[i], moe_w_down[i])
    return h.reshape(B, S, D)
```

```python
import functools

import numpy as np
import jax
import jax.numpy as jnp
from jax import lax
from jax.experimental import pallas as pl
from jax.experimental.pallas import tpu as pltpu

HEAD_DIM = 128
N_HEADS = 16
HEADS_DILATED = 6
HEADS_MOBA = 5
HEADS_STICK = 5
DILATED_PATTERNS = ((128, 1), (512, 4), (2048, 16))
MOBA_BLOCK = 256
MOBA_TOPK = 3
N_EXPERTS = 8
EPS = 1e-6

LANES = 128
VMEM_LIMIT = 56 * 1024 * 1024
MOE_VMEM_LIMIT = 60 * 1024 * 1024
DMA_ISSUE_UNROLL = 8
ATTN_TILE = 256
NEG_INF = float("-inf")
LOG2E = 1.4426950408889634
Q_PRESCALE = HEAD_DIM ** -0.5 * LOG2E

F32 = jnp.float32
BF16 = jnp.bfloat16


def _params(sem, vmem=VMEM_LIMIT):
    return pltpu.CompilerParams(dimension_semantics=sem, vmem_limit_bytes=vmem)


def _dot(a, b):
    return jnp.dot(a, b, preferred_element_type=F32)


def _dot_nt(a, b):
    return lax.dot_general(a, b, (((1,), (1,)), ((), ())), preferred_element_type=F32)


def _rms(x, gain):
    ms = jnp.mean(x * x, axis=-1, keepdims=True)
    return x * lax.rsqrt(ms + EPS) * gain


def _split_bf16(x):
    hi = x.astype(BF16)
    lo = (x - hi.astype(F32)).astype(BF16)
    return hi, lo


def _qkv_kernel(x_ref, g_ref, w_ref, hg_ref, hf_ref, o_ref, a_ref):
    @pl.when(pl.program_id(1) == 0)
    def _():
        a_ref[...] = _rms(x_ref[...], g_ref[...]).astype(BF16)

    y = _dot(a_ref[...], w_ref[...])
    for c in range(y.shape[1] // HEAD_DIM):
        cols = slice(c * HEAD_DIM, (c + 1) * HEAD_DIM)
        blk = y[:, cols]
        ms = jnp.mean(blk * blk, axis=-1, keepdims=True)
        normed = blk * lax.rsqrt(ms + EPS)
        o_ref[:, cols] = (jnp.where(hf_ref[:, cols] > 0, normed, blk) * hg_ref[:, cols]).astype(o_ref.dtype)


def _qkv_proj(h, gain, w_bf16, head_gain, head_flag, *, tm=1024, tn=768):
    T, D = h.shape
    N = w_bf16.shape[1]
    return pl.pallas_call(
        _qkv_kernel,
        grid=(T // tm, N // tn),
        in_specs=[
            pl.BlockSpec((tm, D), lambda i, j: (i, 0)),
            pl.BlockSpec((1, D), lambda i, j: (0, 0)),
            pl.BlockSpec((D, tn), lambda i, j: (0, j)),
            pl.BlockSpec((1, tn), lambda i, j: (0, j)),
            pl.BlockSpec((1, tn), lambda i, j: (0, j)),
        ],
        out_specs=pl.BlockSpec((tm, tn), lambda i, j: (i, j)),
        out_shape=jax.ShapeDtypeStruct((T, N), BF16),
        scratch_shapes=[pltpu.VMEM((tm, D), BF16)],
        compiler_params=_params(("parallel", "arbitrary")),
        name="qkv_proj",
    )(h, gain.reshape(1, D), w_bf16, head_gain, head_flag)


def _attn_specs(S, n_q_tiles, head0, tq):
    q_spec = pl.BlockSpec((tq, HEAD_DIM), lambda b, h, i, *_: (b * n_q_tiles + i, head0 + h))
    k_spec = pl.BlockSpec((S, HEAD_DIM), lambda b, h, i, *_: (b, N_HEADS + head0 + h))
    v_spec = pl.BlockSpec((S, HEAD_DIM), lambda b, h, i, *_: (b, 2 * N_HEADS + head0 + h))
    g_spec = pl.BlockSpec((1, HEAD_DIM), lambda b, h, i, *_: (0, head0 + h))
    o_spec = pl.BlockSpec((tq, HEAD_DIM), lambda b, h, i, *_: (b * n_q_tiles + i, h))
    return q_spec, k_spec, v_spec, g_spec, o_spec


def _key_rows(ref, first_block, n_blocks, tq):
    return ref[pl.ds(pl.multiple_of(first_block * tq, tq), n_blocks * tq), :]


def _lane_blocks(x, tq):
    return [x[:, w * tq:(w + 1) * tq] for w in range(x.shape[1] // tq)]


def _partial_softmax(scores, v_rows):
    m = jnp.max(functools.reduce(jnp.maximum, scores), axis=-1, keepdims=True)
    m_safe = jnp.where(m == NEG_INF, 0.0, m)
    p = [jnp.exp2(s - m_safe) for s in scores]
    l = jnp.sum(functools.reduce(jnp.add, p), axis=-1, keepdims=True)
    return m, l, _dot(jnp.concatenate([x.astype(BF16) for x in p], axis=1), v_rows)


def _merged_softmax(parts):
    m = functools.reduce(jnp.maximum, [part[0] for part in parts])
    weights = [jnp.exp2(part[0] - m) for part in parts]
    l = functools.reduce(jnp.add, [w * part[1] for w, part in zip(weights, parts)])
    acc = functools.reduce(jnp.add, [w * part[2] for w, part in zip(weights, parts)])
    return acc / l


def _dilated_kernel(q_ref, k_ref, v_ref, g_ref, bias_ref, o_ref, *, tq, n_win, split):
    i = pl.program_id(2)
    before = jnp.minimum(i, n_win - 1)
    q = q_ref[...]
    parts = []
    for w0 in range(0, n_win, split):
        n = min(split, n_win - w0)
        s_all = _dot_nt(q, _key_rows(k_ref, i - before + w0, n, tq))
        scores = [s + bias_ref[0, before + n_win - 1 - (w0 + w)]
                  for w, s in enumerate(_lane_blocks(s_all, tq))]
        parts.append(_partial_softmax(scores, _key_rows(v_ref, i - before + w0, n, tq)))
    o_ref[...] = _rms(_merged_softmax(parts), g_ref[...]).astype(o_ref.dtype)


def _dilated_bias(slopes, tq, n_back, n_win):
    d = jnp.arange(n_back + n_win, dtype=jnp.int32)[:, None, None] - (n_win - 1)
    r = jnp.arange(tq, dtype=jnp.int32)[None, :, None]
    c = jnp.arange(tq, dtype=jnp.int32)[None, None, :]
    dist = d * tq + r - c
    mult = jnp.zeros(dist.shape, F32)
    for window, dilation in DILATED_PATTERNS:
        hit = (dist >= 0) & (dist <= window) & (dist % dilation == 0)
        mult = mult + hit.astype(F32)
    logm = jnp.where(mult > 0, jnp.log2(jnp.maximum(mult, 1.0)), NEG_INF)
    return logm[None] - (slopes * LOG2E)[:, None, None, None] * dist.astype(F32)[None]


def _dilated_attention(qkv, out_gain, slopes, B, S, *, tq=ATTN_TILE):
    nq = S // tq
    max_window = max(w for w, _ in DILATED_PATTERNS)
    n_back = -(-max_window // tq)
    n_win = min(n_back + 1, nq)
    bias = _dilated_bias(slopes, tq, n_back, n_win)
    q_spec, k_spec, v_spec, g_spec, o_spec = _attn_specs(S, nq, 0, tq)
    bias_spec = pl.BlockSpec((1, n_back + n_win, tq, tq), lambda b, h, i: (h, 0, 0, 0))
    return pl.pallas_call(
        functools.partial(_dilated_kernel, tq=tq, n_win=n_win, split=n_win),
        grid=(B, HEADS_DILATED, nq),
        in_specs=[q_spec, k_spec, v_spec, g_spec, bias_spec],
        out_specs=o_spec,
        out_shape=jax.ShapeDtypeStruct((B * S, HEADS_DILATED * HEAD_DIM), BF16),
        compiler_params=_params(("parallel", "parallel", "arbitrary")),
        name="dilated_attn",
    )(qkv, qkv, qkv, out_gain, bias)


def _moba_block_bias(q, km_hi, km_lo, own, slope, kb):
    gate = _dot_nt(q, km_hi) + _dot_nt(q, km_lo)
    lane = lax.broadcasted_iota(jnp.int32, gate.shape, 1)
    lane_f = lane.astype(F32)
    g = jnp.where(lane < own, gate, NEG_INF)
    sel = jnp.zeros(gate.shape, jnp.bool_)
    for _ in range(MOBA_TOPK):
        top = jnp.max(g, axis=-1, keepdims=True)
        is_top = (g == top) & (top > NEG_INF)
        first = jnp.min(jnp.where(is_top, lane_f, float(LANES)), axis=-1, keepdims=True)
        pick = lane_f == first
        sel = sel | pick
        g = jnp.where(pick, NEG_INF, g)
    block_alibi = slope * ((own - lane) * kb).astype(F32)
    return jnp.where(lane < own, jnp.where(sel, 0.0, NEG_INF) - block_alibi,
                     jnp.where(lane == own, 0.0, NEG_INF))


def _moba_kernel(slopes_ref, q_ref, q_all_ref, k_ref, v_ref, g_ref, o_ref, kmean_ref, block_bias_ref,
                 *, kb, n_blocks, group):
    h = pl.program_id(1)
    i = pl.program_id(2)
    slope = slopes_ref[h]

    @pl.when(i == 0)
    def _():
        kmean_ref[...] = jnp.zeros_like(kmean_ref)
        for n in range(n_blocks):
            rows = k_ref[n * kb:(n + 1) * kb, :].astype(F32)
            kmean_ref[n:n + 1, :] = jnp.mean(rows, axis=0, keepdims=True)
        km_hi, km_lo = _split_bf16(kmean_ref[...])
        for n in range(n_blocks):
            rows = slice(n * kb, (n + 1) * kb)
            block_bias_ref[rows, :] = _moba_block_bias(q_all_ref[rows, :], km_hi, km_lo, n, slope, kb)

    q = q_ref[...]
    per_block = block_bias_ref[pl.ds(pl.multiple_of(i * kb, kb), kb), :]
    rc = lax.broadcasted_iota(jnp.int32, (kb, kb), 0) - lax.broadcasted_iota(jnp.int32, (kb, kb), 1)
    in_block_alibi = slope * rc.astype(F32)
    own_group = i // group

    def attend(grp):
        rows = slice(grp * group * kb, (grp + 1) * group * kb)
        scores = []
        for w, s in enumerate(_lane_blocks(_dot_nt(q, k_ref[rows, :]), kb)):
            n = grp * group + w
            bias = jnp.broadcast_to(per_block[:, n:n + 1], (kb, kb)) - in_block_alibi
            scores.append(s + jnp.where((n == i) & (rc < 0), NEG_INF, bias))
        return _partial_softmax(scores, v_ref[rows, :])

    for last in range(n_blocks // group):
        @pl.when(own_group == last)
        def _():
            out = _merged_softmax([attend(grp) for grp in range(last + 1)])
            o_ref[...] = _rms(out, g_ref[...]).astype(o_ref.dtype)


def _moba_attention(qkv, out_gain, slopes, B, S, *, kb=MOBA_BLOCK, group=4):
    n_blocks = S // kb
    assert S % kb == 0 and n_blocks <= LANES and n_blocks % group == 0
    head0 = HEADS_DILATED
    q_spec, k_spec, v_spec, g_spec, o_spec = _attn_specs(S, n_blocks, head0, kb)
    q_all_spec = pl.BlockSpec((S, HEAD_DIM), lambda b, h, i: (b, head0 + h))
    return pl.pallas_call(
        functools.partial(_moba_kernel, kb=kb, n_blocks=n_blocks, group=group),
        grid=(B, HEADS_MOBA, n_blocks),
        in_specs=[pl.BlockSpec(memory_space=pltpu.SMEM), q_spec, q_all_spec, k_spec, v_spec, g_spec],
        out_specs=o_spec,
        scratch_shapes=[pltpu.VMEM((LANES, HEAD_DIM), F32), pltpu.VMEM((S, LANES), F32)],
        out_shape=jax.ShapeDtypeStruct((B * S, HEADS_MOBA * HEAD_DIM), BF16),
        compiler_params=_params(("arbitrary", "arbitrary", "arbitrary")),
        name="moba_attn",
    )(slopes * LOG2E, qkv, qkv, qkv, qkv, out_gain)


def _stick_kernel(q_ref, k_ref, v_ref, g_ref, o_ref, *, kb, group):
    i = pl.program_id(2)
    q = q_ref[...]
    tq = q.shape[0]
    rc = lax.broadcasted_iota(jnp.int32, (tq, kb), 0) - lax.broadcasted_iota(jnp.int32, (tq, kb), 1)
    later = (lax.broadcasted_iota(jnp.int32, (kb, kb), 0)
             > lax.broadcasted_iota(jnp.int32, (kb, kb), 1)).astype(BF16)

    def attend(grp, carry, diagonal):
        z_blocks = _lane_blocks(_dot_nt(q, _key_rows(k_ref, grp * group, group, kb)), kb)
        weights = [None] * group
        for w in reversed(range(group)):
            z = z_blocks[w]
            log_1m = jnp.minimum(-z, 0.0) - jnp.log2(1.0 + jnp.exp2(-jnp.abs(z)))
            if diagonal:
                past = rc > w * kb
                log_1m = jnp.where(past, log_1m, 0.0)
            after = _dot(log_1m.astype(BF16), later) + carry
            a = jnp.exp2(z + log_1m + after)
            if diagonal:
                a = jnp.where(past, a, 0.0)
            weights[w] = a.astype(BF16)
            carry = carry + jnp.sum(log_1m, axis=-1, keepdims=True)
        return carry, _dot(jnp.concatenate(weights, axis=1), _key_rows(v_ref, grp * group, group, kb))

    carry, acc = attend(i, jnp.zeros((tq, 1), F32), True)

    odd = i % 2

    def single(state):
        carry, out = attend(i - 1, state[0], False)
        return carry, state[1] + out

    carry, acc = lax.cond(odd == 1, single, lambda state: state, (carry, acc))

    def body(t, state):
        first = i - 1 - odd - 2 * t
        carry, out_a = attend(first, state[0], False)
        carry, out_b = attend(first - 1, carry, False)
        return carry, state[1] + out_a + out_b

    _, acc = lax.fori_loop(0, i // 2, body, (carry, acc))
    o_ref[...] = _rms(acc, g_ref[...]).astype(o_ref.dtype)


def _stick_attention(qkv, out_gain, B, S, *, kb=ATTN_TILE, group=2):
    tq = kb * group
    nq = S // tq
    assert S % tq == 0
    head0 = HEADS_DILATED + HEADS_MOBA
    q_spec, k_spec, v_spec, g_spec, o_spec = _attn_specs(S, nq, head0, tq)
    return pl.pallas_call(
        functools.partial(_stick_kernel, kb=kb, group=group),
        grid=(B, HEADS_STICK, nq),
        in_specs=[q_spec, k_spec, v_spec, g_spec],
        out_specs=o_spec,
        out_shape=jax.ShapeDtypeStruct((B * S, HEADS_STICK * HEAD_DIM), BF16),
        compiler_params=_params(("parallel", "parallel", "arbitrary")),
        name="stick_attn",
    )(qkv, qkv, qkv, out_gain)


def _oproj_kernel(oa_ref, ob_ref, oc_ref, w_ref, h_ref, g_ref, hn_ref, f_ref):
    ka, kb = oa_ref.shape[1], ob_ref.shape[1]
    y = _dot(oa_ref[...], w_ref[0:ka, :])
    y = y + _dot(ob_ref[...], w_ref[ka:ka + kb, :])
    y = y + _dot(oc_ref[...], w_ref[ka + kb:, :])
    hn = h_ref[...] + y
    hn_ref[...] = hn
    f_ref[...] = _rms(hn, g_ref[...]).astype(f_ref.dtype)


def _out_proj(oa, ob, oc, w_bf16, h, gain, f_dtype, *, tm=512):
    T, D = h.shape
    row = lambda i: (i, 0)
    fixed = lambda i: (0, 0)
    return pl.pallas_call(
        _oproj_kernel,
        grid=(T // tm,),
        in_specs=[
            pl.BlockSpec((tm, oa.shape[1]), row),
            pl.BlockSpec((tm, ob.shape[1]), row),
            pl.BlockSpec((tm, oc.shape[1]), row),
            pl.BlockSpec((D, D), fixed),
            pl.BlockSpec((tm, D), row),
            pl.BlockSpec((1, D), fixed),
        ],
        out_specs=[pl.BlockSpec((tm, D), row), pl.BlockSpec((tm, D), row)],
        out_shape=[jax.ShapeDtypeStruct((T, D), F32), jax.ShapeDtypeStruct((T, D), f_dtype)],
        compiler_params=_params(("parallel",)),
        name="out_proj",
    )(oa, ob, oc, w_bf16, h, gain.reshape(1, D))


def _swiglu_block(f, wg_ref, wu_ref, wd_ref):
    g = _dot(f, wg_ref[...])
    u = _dot(f, wu_ref[...])
    a = (g * jax.nn.sigmoid(g) * u).astype(BF16)
    return _dot(a, wd_ref[...])


def _ffn_kernel(f_ref, wg_ref, wu_ref, wd_ref, h_ref, o_ref, acc_ref):
    j = pl.program_id(1)

    @pl.when(j == 0)
    def _():
        acc_ref[...] = h_ref[...]

    acc_ref[...] += _swiglu_block(f_ref[...], wg_ref, wu_ref, wd_ref)

    @pl.when(j == pl.num_programs(1) - 1)
    def _():
        o_ref[...] = acc_ref[...]


def _ffn_dense(f, h, wg, wu, wd, *, tm=512, tf=512):
    T, D = h.shape
    F = wg.shape[1]
    return pl.pallas_call(
        _ffn_kernel,
        grid=(T // tm, F // tf),
        in_specs=[
            pl.BlockSpec((tm, D), lambda i, j: (i, 0)),
            pl.BlockSpec((D, tf), lambda i, j: (0, j)),
            pl.BlockSpec((D, tf), lambda i, j: (0, j)),
            pl.BlockSpec((tf, D), lambda i, j: (j, 0)),
            pl.BlockSpec((tm, D), lambda i, j: (i, 0)),
        ],
        out_specs=pl.BlockSpec((tm, D), lambda i, j: (i, 0)),
        out_shape=jax.ShapeDtypeStruct((T, D), F32),
        scratch_shapes=[pltpu.VMEM((tm, D), F32)],
        compiler_params=_params(("parallel", "arbitrary")),
        name="ffn_dense",
    )(f, wg, wu, wd, h)


META_EXPERT, META_RANK, META_GATE = 0, 2, 4


def _router_kernel(f_ref, w_ref, meta_ref, count_ref):
    @pl.when(pl.program_id(0) == 0)
    def _():
        count_ref[...] = jnp.zeros_like(count_ref)

    logits = _dot(f_ref[...].astype(BF16), w_ref[...])
    tm = logits.shape[0]
    lane = lax.broadcasted_iota(jnp.int32, logits.shape, 1).astype(F32)
    g = jnp.where(lane < N_EXPERTS, logits, NEG_INF)
    picks, tops, experts = [], [], []
    for _ in range(2):
        top = jnp.max(g, axis=-1, keepdims=True)
        first = jnp.min(jnp.where(g == top, lane, float(LANES)), axis=-1, keepdims=True)
        pick = lane == first
        g = jnp.where(pick, NEG_INF, g)
        picks.append(pick)
        tops.append(top)
        experts.append(first)
    e2 = jnp.exp(tops[1] - tops[0])
    denom = 1.0 + e2
    gates = [1.0 / denom, e2 / denom]

    chosen = (picks[0] | picks[1]).astype(BF16)
    r = lax.broadcasted_iota(jnp.int32, (tm, tm), 0)
    c = lax.broadcasted_iota(jnp.int32, (tm, tm), 1)
    before = _dot((r > c).astype(BF16), chosen) + count_ref[...]
    ranks = [jnp.sum(jnp.where(p, before, 0.0), axis=-1, keepdims=True) for p in picks]
    count_ref[...] += jnp.sum(chosen.astype(F32), axis=0, keepdims=True)

    meta = jnp.zeros(logits.shape, F32)
    for base, pair in ((META_EXPERT, experts), (META_RANK, ranks), (META_GATE, gates)):
        for s in range(2):
            meta = jnp.where(lane == float(base + s), pair[s], meta)
    meta_ref[...] = meta


def _router(f, router_w, *, tm=512):
    T, D = f.shape
    w = jnp.zeros((D, LANES), BF16).at[:, :N_EXPERTS].set(router_w.astype(BF16))
    return pl.pallas_call(
        _router_kernel,
        grid=(T // tm,),
        in_specs=[pl.BlockSpec((tm, D), lambda i: (i, 0)), pl.BlockSpec((D, LANES), lambda i: (0, 0))],
        out_specs=[pl.BlockSpec((tm, LANES), lambda i: (i, 0)), pl.BlockSpec((1, LANES), lambda i: (0, 0))],
        out_shape=[jax.ShapeDtypeStruct((T, LANES), F32), jax.ShapeDtypeStruct((1, LANES), F32)],
        compiler_params=_params(("arbitrary",)),
        name="moe_router",
    )(f, w)


def _dispatch_kernel(pos_ref, f_ref, xs_init_ref, xs_ref, sem):
    del xs_init_ref
    tt = f_ref.shape[0]

    def issue(r, carry):
        src = f_ref.at[pl.ds(r, 1)]
        for s in range(2):
            pltpu.make_async_copy(src, xs_ref.at[pl.ds(pos_ref[0, 0, 2 * r + s], 1)], sem).start(priority=s)
        return carry

    lax.fori_loop(0, tt, issue, 0, unroll=DMA_ISSUE_UNROLL)
    for _ in range(2):
        pltpu.make_async_copy(f_ref, xs_ref.at[pl.ds(0, tt)], sem).wait()


def _dispatch(pos3, f, n_rows):
    T, D = f.shape
    n_tiles, _, n_slots = pos3.shape
    return pl.pallas_call(
        _dispatch_kernel,
        grid=(n_tiles,),
        in_specs=[
            pl.BlockSpec((1, 1, n_slots), lambda i: (i, 0, 0), memory_space=pltpu.SMEM),
            pl.BlockSpec((n_slots // 2, D), lambda i: (i, 0)),
            pl.BlockSpec(memory_space=pl.ANY),
        ],
        out_specs=pl.BlockSpec(memory_space=pl.ANY),
        out_shape=jax.ShapeDtypeStruct((n_rows, D), f.dtype),
        scratch_shapes=[pltpu.SemaphoreType.DMA(())],
        input_output_aliases={2: 0},
        compiler_params=_params(("arbitrary",)),
        name="moe_dispatch",
    )(pos3, f, jnp.zeros((n_rows, D), f.dtype))


def _moe_group_kernel(te_ref, nu_ref, x_ref, wg_ref, wu_ref, wd_ref, y_ref, xb_ref):
    del te_ref
    g = pl.program_id(0)
    j = pl.program_id(1)
    used = g < nu_ref[0]

    @pl.when(used & (j == 0))
    def _():
        xb_ref[...] = x_ref[...].astype(BF16)

    @pl.when(used)
    def _():
        x = xb_ref[...]
        gate = _dot(x, wg_ref[0].astype(BF16))
        up = _dot(x, wu_ref[0].astype(BF16))
        a = (gate * jax.nn.sigmoid(gate) * up).astype(BF16)
        part = _dot(a, wd_ref[0].astype(BF16))

        @pl.when(j == 0)
        def _():
            y_ref[...] = part

        @pl.when(j > 0)
        def _():
            y_ref[...] += part

    @pl.when(jnp.logical_not(used) & (j == 0))
    def _():
        y_ref[...] = jnp.zeros_like(y_ref)


def _moe_grouped_ffn(tile_expert, n_used, xs, wg, wu, wd, *, tmg, tf):
    P, D = xs.shape
    E, _, F = wg.shape
    J = F // tf

    def f_block(g, j, nu):
        return jnp.where(g < nu[0], j, J - 1)

    grid_spec = pltpu.PrefetchScalarGridSpec(
        num_scalar_prefetch=2,
        grid=(P // tmg, J),
        in_specs=[
            pl.BlockSpec((tmg, D), lambda g, j, te, nu: (jnp.minimum(g, nu[0] - 1), 0)),
            pl.BlockSpec((1, D, tf), lambda g, j, te, nu: (te[g], 0, f_block(g, j, nu))),
            pl.BlockSpec((1, D, tf), lambda g, j, te, nu: (te[g], 0, f_block(g, j, nu))),
            pl.BlockSpec((1, tf, D), lambda g, j, te, nu: (te[g], f_block(g, j, nu), 0)),
        ],
        out_specs=pl.BlockSpec((tmg, D), lambda g, j, te, nu: (g, 0)),
        scratch_shapes=[pltpu.VMEM((tmg, D), BF16)],
    )
    return pl.pallas_call(
        _moe_group_kernel,
        grid_spec=grid_spec,
        out_shape=jax.ShapeDtypeStruct((P, D), F32),
        compiler_params=_params(("arbitrary", "arbitrary"), vmem=MOE_VMEM_LIMIT),
        name="moe_ffn",
    )(tile_expert, n_used, xs, wg, wu, wd)


def _combine_kernel(pos_ref, y_ref, h_ref, meta_ref, o_ref, buf_ref, sem):
    tt = h_ref.shape[0]

    def issue(r, carry):
        for s in range(2):
            pltpu.make_async_copy(y_ref.at[pl.ds(pos_ref[0, 0, 2 * r + s], 1)],
                                  buf_ref.at[s, pl.ds(r, 1)], sem).start(priority=s)
        return carry

    lax.fori_loop(0, tt, issue, 0, unroll=DMA_ISSUE_UNROLL)
    for s in range(2):
        pltpu.make_async_copy(y_ref.at[pl.ds(0, tt)], buf_ref.at[s], sem).wait()
    meta = meta_ref[...]
    o_ref[...] = (h_ref[...] + meta[:, META_GATE:META_GATE + 1] * buf_ref[0]
                  + meta[:, META_GATE + 1:META_GATE + 2] * buf_ref[1])


def _combine(pos3, y, h, meta):
    T, D = h.shape
    n_tiles, _, n_slots = pos3.shape
    tt = n_slots // 2
    return pl.pallas_call(
        _combine_kernel,
        grid=(n_tiles,),
        in_specs=[
            pl.BlockSpec((1, 1, n_slots), lambda i: (i, 0, 0), memory_space=pltpu.SMEM),
            pl.BlockSpec(memory_space=pl.ANY),
            pl.BlockSpec((tt, D), lambda i: (i, 0)),
            pl.BlockSpec((tt, LANES), lambda i: (i, 0)),
        ],
        out_specs=pl.BlockSpec((tt, D), lambda i: (i, 0)),
        out_shape=jax.ShapeDtypeStruct((T, D), F32),
        scratch_shapes=[pltpu.VMEM((2, tt, D), F32), pltpu.SemaphoreType.DMA(())],
        compiler_params=_params(("arbitrary",)),
        name="moe_combine",
    )(pos3, y, h, meta)


def _moe_routed(f, h, router_w, wg, wu, wd, *, tmg=768, tf=512, tt=256):
    T, D = h.shape
    E = wg.shape[0]
    n_tiles = 2 * T // tmg + E
    meta, counts = _router(f, router_w)
    counts = counts[0, :E].astype(jnp.int32)
    tiles_per_expert = (counts + tmg - 1) // tmg
    tile_end = jnp.cumsum(tiles_per_expert)
    row_start = (tile_end - tiles_per_expert) * tmg
    n_used = tile_end[-1:]
    tile_id = jnp.minimum(jnp.arange(n_tiles, dtype=jnp.int32), n_used - 1)
    tile_expert = jnp.sum(tile_id[:, None] >= tile_end[None, :], axis=1).astype(jnp.int32)
    expert = meta[:, META_EXPERT:META_EXPERT + 2].astype(jnp.int32)
    rank = meta[:, META_RANK:META_RANK + 2].astype(jnp.int32)
    pos3 = (row_start[expert] + rank).reshape(T // tt, 1, 2 * tt)
    xs = _dispatch(pos3, f, n_tiles * tmg)
    y = _moe_grouped_ffn(tile_expert, n_used.astype(jnp.int32), xs, wg, wu, wd, tmg=tmg, tf=tf)
    return _combine(pos3, y, h, meta)


def _alibi_slopes():
    n = HEADS_DILATED + HEADS_MOBA
    s = jnp.asarray(2.0 ** (-8.0 * np.arange(1, n + 1) / n), dtype=F32)
    return s[:HEADS_DILATED], s[HEADS_DILATED:]


def _head_norm_rows(q_gain, k_gain, D):
    a = HEADS_DILATED * HEAD_DIM
    b = HEADS_MOBA * HEAD_DIM
    rest = D - a - b

    def row(g):
        return jnp.concatenate([jnp.tile(g[0], HEADS_DILATED), jnp.tile(g[1], HEADS_MOBA), jnp.ones((rest,), F32)])

    flag = jnp.concatenate([jnp.ones((a + b,), F32), jnp.zeros((rest,), F32)])
    gain = jnp.concatenate([row(q_gain) * Q_PRESCALE, row(k_gain), jnp.ones((D,), F32)])
    flags = jnp.concatenate([flag, flag, jnp.zeros((D,), F32)])
    return gain.reshape(1, 3 * D), flags.reshape(1, 3 * D)


def kernel(x, attn_norm, w_in, q_gain, k_gain, out_gain, w_out, ffn_norm, dense_w_gate, dense_w_up, dense_w_down, moe_router, moe_w_gate, moe_w_up, moe_w_down):
    B, S, D = x.shape
    depth = w_in.shape[0]
    slopes_a, slopes_b = _alibi_slopes()
    h = x.reshape(B * S, D)
    for layer in range(depth):
        head_gain, head_flag = _head_norm_rows(q_gain[layer], k_gain[layer], D)
        qkv = _qkv_proj(h, attn_norm[layer], w_in[layer].astype(BF16), head_gain, head_flag)
        og = out_gain[layer].reshape(1, D)
        oa = _dilated_attention(qkv, og, slopes_a, B, S)
        ob = _moba_attention(qkv, og, slopes_b, B, S)
        oc = _stick_attention(qkv, og, B, S)
        dense = layer % 2 == 0
        h, f = _out_proj(oa, ob, oc, w_out[layer].astype(BF16), h, ffn_norm[layer], BF16 if dense else F32)
        i = layer // 2
        if dense:
            h = _ffn_dense(f, h, dense_w_gate[i].astype(BF16), dense_w_up[i].astype(BF16),
                           dense_w_down[i].astype(BF16))
        else:
            h = _moe_routed(f, h, moe_router[i], moe_w_gate[i], moe_w_up[i], moe_w_down[i])
    return h.reshape(B, S, D)
```

```python
import functools

import numpy as np
import jax
import jax.numpy as jnp
from jax import lax
from jax.experimental import pallas as pl
from jax.experimental.pallas import tpu as pltpu

HEAD_DIM = 128
N_HEADS = 16
HEADS_DILATED = 6
HEADS_MOBA = 5
HEADS_STICK = 5
DILATED_PATTERNS = ((128, 1), (512, 4), (2048, 16))
MOBA_BLOCK = 256
MOBA_TOPK = 3
N_EXPERTS = 8
EPS = 1e-6

LANES = 128
VMEM_LIMIT = 56 * 1024 * 1024
MOE_VMEM_LIMIT = 60 * 1024 * 1024
DMA_ISSUE_UNROLL = 8
ATTN_TILE = 256
NEG_INF = float("-inf")
LOG2E = 1.4426950408889634
Q_PRESCALE = HEAD_DIM ** -0.5 * LOG2E

F32 = jnp.float32
BF16 = jnp.bfloat16


def _params(sem, vmem=VMEM_LIMIT):
    return pltpu.CompilerParams(dimension_semantics=sem, vmem_limit_bytes=vmem)


def _dot(a, b):
    return jnp.dot(a, b, preferred_element_type=F32)


def _dot_nt(a, b):
    return lax.dot_general(a, b, (((1,), (1,)), ((), ())), preferred_element_type=F32)


def _rms(x, gain):
    ms = jnp.mean(x * x, axis=-1, keepdims=True)
    return x * lax.rsqrt(ms + EPS) * gain


def _split_bf16(x):
    hi = x.astype(BF16)
    lo = (x - hi.astype(F32)).astype(BF16)
    return hi, lo


def _qkv_kernel(x_ref, g_ref, w_ref, hg_ref, hf_ref, o_ref, a_ref):
    @pl.when(pl.program_id(1) == 0)
    def _():
        a_ref[...] = _rms(x_ref[...], g_ref[...]).astype(BF16)

    y = _dot(a_ref[...], w_ref[...])
    for c in range(y.shape[1] // HEAD_DIM):
        cols = slice(c * HEAD_DIM, (c + 1) * HEAD_DIM)
        blk = y[:, cols]
        ms = jnp.mean(blk * blk, axis=-1, keepdims=True)
        normed = blk * lax.rsqrt(ms + EPS)
        o_ref[:, cols] = (jnp.where(hf_ref[:, cols] > 0, normed, blk) * hg_ref[:, cols]).astype(o_ref.dtype)


def _qkv_proj(h, gain, w_bf16, head_gain, head_flag, *, tm=1024, tn=768):
    T, D = h.shape
    N = w_bf16.shape[1]
    return pl.pallas_call(
        _qkv_kernel,
        grid=(T // tm, N // tn),
        in_specs=[
            pl.BlockSpec((tm, D), lambda i, j: (i, 0)),
            pl.BlockSpec((1, D), lambda i, j: (0, 0)),
            pl.BlockSpec((D, tn), lambda i, j: (0, j)),
            pl.BlockSpec((1, tn), lambda i, j: (0, j)),
            pl.BlockSpec((1, tn), lambda i, j: (0, j)),
        ],
        out_specs=pl.BlockSpec((tm, tn), lambda i, j: (i, j)),
        out_shape=jax.ShapeDtypeStruct((T, N), BF16),
        scratch_shapes=[pltpu.VMEM((tm, D), BF16)],
        compiler_params=_params(("parallel", "arbitrary")),
        name="qkv_proj",
    )(h, gain.reshape(1, D), w_bf16, head_gain, head_flag)


def _attn_specs(S, n_q_tiles, head0, tq):
    q_spec = pl.BlockSpec((tq, HEAD_DIM), lambda b, h, i, *_: (b * n_q_tiles + i, head0 + h))
    k_spec = pl.BlockSpec((S, HEAD_DIM), lambda b, h, i, *_: (b, N_HEADS + head0 + h))
    v_spec = pl.BlockSpec((S, HEAD_DIM), lambda b, h, i, *_: (b, 2 * N_HEADS + head0 + h))
    g_spec = pl.BlockSpec((1, HEAD_DIM), lambda b, h, i, *_: (0, head0 + h))
    o_spec = pl.BlockSpec((tq, HEAD_DIM), lambda b, h, i, *_: (b * n_q_tiles + i, h))
    return q_spec, k_spec, v_spec, g_spec, o_spec


def _side_cast_specs(w2d, grid):
    rows, cols = w2d.shape
    steps = int(np.prod(grid))
    n_blocks = max(n for n in range(1, steps + 1) if rows % n == 0 and (rows // n) % 16 == 0)

    def index(*ids):
        step = ids[0]
        for size, idx in zip(grid[1:], ids[1:]):
            step = step * size + idx
        return jnp.minimum(step, n_blocks - 1), 0

    spec = pl.BlockSpec((rows // n_blocks, cols), index)
    return spec, spec, jax.ShapeDtypeStruct(w2d.shape, BF16)


def _side_cast(w_ref, wb_ref):
    wb_ref[...] = w_ref[...].astype(BF16)


def _key_rows(ref, first_block, n_blocks, tq):
    return ref[pl.ds(pl.multiple_of(first_block * tq, tq), n_blocks * tq), :]


def _lane_blocks(x, tq):
    return [x[:, w * tq:(w + 1) * tq] for w in range(x.shape[1] // tq)]


def _partial_softmax(scores, v_rows):
    m = jnp.max(functools.reduce(jnp.maximum, scores), axis=-1, keepdims=True)
    m_safe = jnp.where(m == NEG_INF, 0.0, m)
    p = [jnp.exp2(s - m_safe) for s in scores]
    l = jnp.sum(functools.reduce(jnp.add, p), axis=-1, keepdims=True)
    return m, l, _dot(jnp.concatenate([x.astype(BF16) for x in p], axis=1), v_rows)


def _merged_softmax(parts):
    m = functools.reduce(jnp.maximum, [part[0] for part in parts])
    weights = [jnp.exp2(part[0] - m) for part in parts]
    l = functools.reduce(jnp.add, [w * part[1] for w, part in zip(weights, parts)])
    acc = functools.reduce(jnp.add, [w * part[2] for w, part in zip(weights, parts)])
    return acc / l


def _dilated_kernel(q_ref, k_ref, v_ref, g_ref, bias_ref, w_ref, o_ref, wb_ref, *, tq, n_win, split):
    _side_cast(w_ref, wb_ref)
    i = pl.program_id(2)
    before = jnp.minimum(i, n_win - 1)
    q = q_ref[...]
    parts = []
    for w0 in range(0, n_win, split):
        n = min(split, n_win - w0)
        s_all = _dot_nt(q, _key_rows(k_ref, i - before + w0, n, tq))
        scores = [s + bias_ref[0, before + n_win - 1 - (w0 + w)]
                  for w, s in enumerate(_lane_blocks(s_all, tq))]
        parts.append(_partial_softmax(scores, _key_rows(v_ref, i - before + w0, n, tq)))
    o_ref[...] = _rms(_merged_softmax(parts), g_ref[...]).astype(o_ref.dtype)


def _dilated_bias(slopes, tq, n_back, n_win):
    d = jnp.arange(n_back + n_win, dtype=jnp.int32)[:, None, None] - (n_win - 1)
    r = jnp.arange(tq, dtype=jnp.int32)[None, :, None]
    c = jnp.arange(tq, dtype=jnp.int32)[None, None, :]
    dist = d * tq + r - c
    mult = jnp.zeros(dist.shape, F32)
    for window, dilation in DILATED_PATTERNS:
        hit = (dist >= 0) & (dist <= window) & (dist % dilation == 0)
        mult = mult + hit.astype(F32)
    logm = jnp.where(mult > 0, jnp.log2(jnp.maximum(mult, 1.0)), NEG_INF)
    return logm[None] - (slopes * LOG2E)[:, None, None, None] * dist.astype(F32)[None]


def _dilated_attention(qkv, out_gain, slopes, w2d, B, S, *, tq=ATTN_TILE):
    nq = S // tq
    max_window = max(w for w, _ in DILATED_PATTERNS)
    n_back = -(-max_window // tq)
    n_win = min(n_back + 1, nq)
    bias = _dilated_bias(slopes, tq, n_back, n_win)
    grid = (B, HEADS_DILATED, nq)
    q_spec, k_spec, v_spec, g_spec, o_spec = _attn_specs(S, nq, 0, tq)
    bias_spec = pl.BlockSpec((1, n_back + n_win, tq, tq), lambda b, h, i: (h, 0, 0, 0))
    w_spec, wb_spec, wb_shape = _side_cast_specs(w2d, grid)
    return pl.pallas_call(
        functools.partial(_dilated_kernel, tq=tq, n_win=n_win, split=n_win),
        grid=grid,
        in_specs=[q_spec, k_spec, v_spec, g_spec, bias_spec, w_spec],
        out_specs=[o_spec, wb_spec],
        out_shape=[jax.ShapeDtypeStruct((B * S, HEADS_DILATED * HEAD_DIM), BF16), wb_shape],
        compiler_params=_params(("arbitrary", "arbitrary", "arbitrary")),
        name="dilated_attn",
    )(qkv, qkv, qkv, out_gain, bias, w2d)


def _moba_block_bias(q, km_hi, km_lo, own, slope, kb):
    gate = _dot_nt(q, km_hi) + _dot_nt(q, km_lo)
    lane = lax.broadcasted_iota(jnp.int32, gate.shape, 1)
    lane_f = lane.astype(F32)
    g = jnp.where(lane < own, gate, NEG_INF)
    sel = jnp.zeros(gate.shape, jnp.bool_)
    for _ in range(MOBA_TOPK):
        top = jnp.max(g, axis=-1, keepdims=True)
        is_top = (g == top) & (top > NEG_INF)
        first = jnp.min(jnp.where(is_top, lane_f, float(LANES)), axis=-1, keepdims=True)
        pick = lane_f == first
        sel = sel | pick
        g = jnp.where(pick, NEG_INF, g)
    block_alibi = slope * ((own - lane) * kb).astype(F32)
    return jnp.where(lane < own, jnp.where(sel, 0.0, NEG_INF) - block_alibi,
                     jnp.where(lane == own, 0.0, NEG_INF))


def _moba_kernel(slopes_ref, q_ref, q_all_ref, k_ref, v_ref, g_ref, w_ref, o_ref, wb_ref,
                 kmean_ref, block_bias_ref, *, kb, n_blocks, group):
    h = pl.program_id(1)
    i = pl.program_id(2)
    slope = slopes_ref[h]

    @pl.when(i == 0)
    def _():
        kmean_ref[...] = jnp.zeros_like(kmean_ref)
        for n in range(n_blocks):
            rows = k_ref[n * kb:(n + 1) * kb, :].astype(F32)
            kmean_ref[n:n + 1, :] = jnp.mean(rows, axis=0, keepdims=True)
        km_hi, km_lo = _split_bf16(kmean_ref[...])
        for n in range(n_blocks):
            rows = slice(n * kb, (n + 1) * kb)
            block_bias_ref[rows, :] = _moba_block_bias(q_all_ref[rows, :], km_hi, km_lo, n, slope, kb)

    q = q_ref[...]
    per_block = block_bias_ref[pl.ds(pl.multiple_of(i * kb, kb), kb), :]
    rc = lax.broadcasted_iota(jnp.int32, (kb, kb), 0) - lax.broadcasted_iota(jnp.int32, (kb, kb), 1)
    in_block_alibi = slope * rc.astype(F32)
    own_group = i // group

    def attend(grp):
        rows = slice(grp * group * kb, (grp + 1) * group * kb)
        scores = []
        for w, s in enumerate(_lane_blocks(_dot_nt(q, k_ref[rows, :]), kb)):
            n = grp * group + w
            bias = jnp.broadcast_to(per_block[:, n:n + 1], (kb, kb)) - in_block_alibi
            scores.append(s + jnp.where((n == i) & (rc < 0), NEG_INF, bias))
        return _partial_softmax(scores, v_ref[rows, :])

    for last in range(n_blocks // group):
        @pl.when(own_group == last)
        def _():
            _side_cast(w_ref, wb_ref)
            out = _merged_softmax([attend(grp) for grp in range(last + 1)])
            o_ref[...] = _rms(out, g_ref[...]).astype(o_ref.dtype)


def _moba_attention(qkv, out_gain, slopes, w2d, B, S, *, kb=MOBA_BLOCK, group=4):
    n_blocks = S // kb
    assert S % kb == 0 and n_blocks <= LANES and n_blocks % group == 0
    head0 = HEADS_DILATED
    grid = (B, HEADS_MOBA, n_blocks)
    q_spec, k_spec, v_spec, g_spec, o_spec = _attn_specs(S, n_blocks, head0, kb)
    q_all_spec = pl.BlockSpec((S, HEAD_DIM), lambda b, h, i: (b, head0 + h))
    w_spec, wb_spec, wb_shape = _side_cast_specs(w2d, grid)
    return pl.pallas_call(
        functools.partial(_moba_kernel, kb=kb, n_blocks=n_blocks, group=group),
        grid=grid,
        in_specs=[pl.BlockSpec(memory_space=pltpu.SMEM), q_spec, q_all_spec, k_spec, v_spec, g_spec, w_spec],
        out_specs=[o_spec, wb_spec],
        scratch_shapes=[pltpu.VMEM((LANES, HEAD_DIM), F32), pltpu.VMEM((S, LANES), F32)],
        out_shape=[jax.ShapeDtypeStruct((B * S, HEADS_MOBA * HEAD_DIM), BF16), wb_shape],
        compiler_params=_params(("arbitrary", "arbitrary", "arbitrary")),
        name="moba_attn",
    )(slopes * LOG2E, qkv, qkv, qkv, qkv, out_gain, w2d)


def _stick_kernel(q_ref, k_ref, v_ref, g_ref, w_ref, o_ref, wb_ref, *, kb, group):
    _side_cast(w_ref, wb_ref)
    i = pl.program_id(2)
    q = q_ref[...]
    tq = q.shape[0]
    rc = lax.broadcasted_iota(jnp.int32, (tq, kb), 0) - lax.broadcasted_iota(jnp.int32, (tq, kb), 1)
    later = (lax.broadcasted_iota(jnp.int32, (kb, kb), 0)
             > lax.broadcasted_iota(jnp.int32, (kb, kb), 1)).astype(BF16)

    def attend(grp, carry, diagonal):
        z_blocks = _lane_blocks(_dot_nt(q, _key_rows(k_ref, grp * group, group, kb)), kb)
        weights = [None] * group
        for w in reversed(range(group)):
            z = z_blocks[w]
            log_1m = jnp.minimum(-z, 0.0) - jnp.log2(1.0 + jnp.exp2(-jnp.abs(z)))
            if diagonal:
                past = rc > w * kb
                log_1m = jnp.where(past, log_1m, 0.0)
            after = _dot(log_1m.astype(BF16), later) + carry
            a = jnp.exp2(z + log_1m + after)
            if diagonal:
                a = jnp.where(past, a, 0.0)
            weights[w] = a.astype(BF16)
            carry = carry + jnp.sum(log_1m, axis=-1, keepdims=True)
        return carry, _dot(jnp.concatenate(weights, axis=1), _key_rows(v_ref, grp * group, group, kb))

    carry, acc = attend(i, jnp.zeros((tq, 1), F32), True)

    odd = i % 2

    def single(state):
        carry, out = attend(i - 1, state[0], False)
        return carry, state[1] + out

    carry, acc = lax.cond(odd == 1, single, lambda state: state, (carry, acc))

    def body(t, state):
        first = i - 1 - odd - 2 * t
        carry, out_a = attend(first, state[0], False)
        carry, out_b = attend(first - 1, carry, False)
        return carry, state[1] + out_a + out_b

    _, acc = lax.fori_loop(0, i // 2, body, (carry, acc))
    o_ref[...] = _rms(acc, g_ref[...]).astype(o_ref.dtype)


def _stick_attention(qkv, out_gain, w2d, B, S, *, kb=ATTN_TILE, group=2):
    tq = kb * group
    nq = S // tq
    assert S % tq == 0
    head0 = HEADS_DILATED + HEADS_MOBA
    grid = (B, HEADS_STICK, nq)
    q_spec, k_spec, v_spec, g_spec, o_spec = _attn_specs(S, nq, head0, tq)
    w_spec, wb_spec, wb_shape = _side_cast_specs(w2d, grid)
    return pl.pallas_call(
        functools.partial(_stick_kernel, kb=kb, group=group),
        grid=grid,
        in_specs=[q_spec, k_spec, v_spec, g_spec, w_spec],
        out_specs=[o_spec, wb_spec],
        out_shape=[jax.ShapeDtypeStruct((B * S, HEADS_STICK * HEAD_DIM), BF16), wb_shape],
        compiler_params=_params(("arbitrary", "arbitrary", "arbitrary")),
        name="stick_attn",
    )(qkv, qkv, qkv, out_gain, w2d)


def _oproj_kernel(oa_ref, ob_ref, oc_ref, w_ref, h_ref, g_ref, hn_ref, f_ref):
    ka, kb = oa_ref.shape[1], ob_ref.shape[1]
    y = _dot(oa_ref[...], w_ref[0:ka, :])
    y = y + _dot(ob_ref[...], w_ref[ka:ka + kb, :])
    y = y + _dot(oc_ref[...], w_ref[ka + kb:, :])
    hn = h_ref[...] + y
    hn_ref[...] = hn
    f_ref[...] = _rms(hn, g_ref[...]).astype(f_ref.dtype)


def _out_proj(oa, ob, oc, w_bf16, h, gain, f_dtype, *, tm=512):
    T, D = h.shape
    row = lambda i: (i, 0)
    fixed = lambda i: (0, 0)
    return pl.pallas_call(
        _oproj_kernel,
        grid=(T // tm,),
        in_specs=[
            pl.BlockSpec((tm, oa.shape[1]), row),
            pl.BlockSpec((tm, ob.shape[1]), row),
            pl.BlockSpec((tm, oc.shape[1]), row),
            pl.BlockSpec((D, D), fixed),
            pl.BlockSpec((tm, D), row),
            pl.BlockSpec((1, D), fixed),
        ],
        out_specs=[pl.BlockSpec((tm, D), row), pl.BlockSpec((tm, D), row)],
        out_shape=[jax.ShapeDtypeStruct((T, D), F32), jax.ShapeDtypeStruct((T, D), f_dtype)],
        compiler_params=_params(("parallel",)),
        name="out_proj",
    )(oa, ob, oc, w_bf16, h, gain.reshape(1, D))


def _swiglu_block(f, wg_ref, wu_ref, wd_ref):
    g = _dot(f, wg_ref[...])
    u = _dot(f, wu_ref[...])
    a = (g * jax.nn.sigmoid(g) * u).astype(BF16)
    return _dot(a, wd_ref[...])


def _ffn_kernel(f_ref, wg_ref, wu_ref, wd_ref, h_ref, o_ref, acc_ref):
    j = pl.program_id(1)

    @pl.when(j == 0)
    def _():
        acc_ref[...] = h_ref[...]

    acc_ref[...] += _swiglu_block(f_ref[...], wg_ref, wu_ref, wd_ref)

    @pl.when(j == pl.num_programs(1) - 1)
    def _():
        o_ref[...] = acc_ref[...]


def _ffn_dense(f, h, wg, wu, wd, *, tm=512, tf=512):
    T, D = h.shape
    F = wg.shape[1]
    return pl.pallas_call(
        _ffn_kernel,
        grid=(T // tm, F // tf),
        in_specs=[
            pl.BlockSpec((tm, D), lambda i, j: (i, 0)),
            pl.BlockSpec((D, tf), lambda i, j: (0, j)),
            pl.BlockSpec((D, tf), lambda i, j: (0, j)),
            pl.BlockSpec((tf, D), lambda i, j: (j, 0)),
            pl.BlockSpec((tm, D), lambda i, j: (i, 0)),
        ],
        out_specs=pl.BlockSpec((tm, D), lambda i, j: (i, 0)),
        out_shape=jax.ShapeDtypeStruct((T, D), F32),
        scratch_shapes=[pltpu.VMEM((tm, D), F32)],
        compiler_params=_params(("parallel", "arbitrary")),
        name="ffn_dense",
    )(f, wg, wu, wd, h)


META_EXPERT, META_RANK, META_GATE = 0, 2, 4


def _router_kernel(f_ref, w_ref, meta_ref, count_ref):
    @pl.when(pl.program_id(0) == 0)
    def _():
        count_ref[...] = jnp.zeros_like(count_ref)

    logits = _dot(f_ref[...].astype(BF16), w_ref[...])
    tm = logits.shape[0]
    lane = lax.broadcasted_iota(jnp.int32, logits.shape, 1).astype(F32)
    g = jnp.where(lane < N_EXPERTS, logits, NEG_INF)
    picks, tops, experts = [], [], []
    for _ in range(2):
        top = jnp.max(g, axis=-1, keepdims=True)
        first = jnp.min(jnp.where(g == top, lane, float(LANES)), axis=-1, keepdims=True)
        pick = lane == first
        g = jnp.where(pick, NEG_INF, g)
        picks.append(pick)
        tops.append(top)
        experts.append(first)
    e2 = jnp.exp(tops[1] - tops[0])
    denom = 1.0 + e2
    gates = [1.0 / denom, e2 / denom]

    chosen = (picks[0] | picks[1]).astype(BF16)
    r = lax.broadcasted_iota(jnp.int32, (tm, tm), 0)
    c = lax.broadcasted_iota(jnp.int32, (tm, tm), 1)
    before = _dot((r > c).astype(BF16), chosen) + count_ref[...]
    ranks = [jnp.sum(jnp.where(p, before, 0.0), axis=-1, keepdims=True) for p in picks]
    count_ref[...] += jnp.sum(chosen.astype(F32), axis=0, keepdims=True)

    meta = jnp.zeros(logits.shape, F32)
    for base, pair in ((META_EXPERT, experts), (META_RANK, ranks), (META_GATE, gates)):
        for s in range(2):
            meta = jnp.where(lane == float(base + s), pair[s], meta)
    meta_ref[...] = meta


def _router(f, router_w, *, tm=512):
    T, D = f.shape
    w = jnp.zeros((D, LANES), BF16).at[:, :N_EXPERTS].set(router_w.astype(BF16))
    return pl.pallas_call(
        _router_kernel,
        grid=(T // tm,),
        in_specs=[pl.BlockSpec((tm, D), lambda i: (i, 0)), pl.BlockSpec((D, LANES), lambda i: (0, 0))],
        out_specs=[pl.BlockSpec((tm, LANES), lambda i: (i, 0)), pl.BlockSpec((1, LANES), lambda i: (0, 0))],
        out_shape=[jax.ShapeDtypeStruct((T, LANES), F32), jax.ShapeDtypeStruct((1, LANES), F32)],
        compiler_params=_params(("arbitrary",)),
        name="moe_router",
    )(f, w)


def _dispatch_kernel(pos_ref, f_ref, xs_init_ref, xs_ref, sem):
    del xs_init_ref
    tt = f_ref.shape[0]

    def issue(r, carry):
        src = f_ref.at[pl.ds(r, 1)]
        for s in range(2):
            pltpu.make_async_copy(src, xs_ref.at[pl.ds(pos_ref[0, 0, 2 * r + s], 1)], sem).start(priority=s)
        return carry

    lax.fori_loop(0, tt, issue, 0, unroll=DMA_ISSUE_UNROLL)
    for _ in range(2):
        pltpu.make_async_copy(f_ref, xs_ref.at[pl.ds(0, tt)], sem).wait()


def _dispatch(pos3, f, n_rows):
    T, D = f.shape
    n_tiles, _, n_slots = pos3.shape
    return pl.pallas_call(
        _dispatch_kernel,
        grid=(n_tiles,),
        in_specs=[
            pl.BlockSpec((1, 1, n_slots), lambda i: (i, 0, 0), memory_space=pltpu.SMEM),
            pl.BlockSpec((n_slots // 2, D), lambda i: (i, 0)),
            pl.BlockSpec(memory_space=pl.ANY),
        ],
        out_specs=pl.BlockSpec(memory_space=pl.ANY),
        out_shape=jax.ShapeDtypeStruct((n_rows, D), f.dtype),
        scratch_shapes=[pltpu.SemaphoreType.DMA(())],
        input_output_aliases={2: 0},
        compiler_params=_params(("arbitrary",)),
        name="moe_dispatch",
    )(pos3, f, jnp.zeros((n_rows, D), f.dtype))


def _moe_group_kernel(te_ref, nu_ref, x_ref, wg_ref, wu_ref, wd_ref, y_ref, xb_ref):
    del te_ref
    g = pl.program_id(0)
    j = pl.program_id(1)
    used = g < nu_ref[0]

    @pl.when(j == 0)
    def _():
        y_ref[...] = jnp.zeros_like(y_ref)
        xb_ref[...] = x_ref[...].astype(BF16)

    @pl.when(used)
    def _():
        y_ref[...] += _swiglu_block(xb_ref[...], wg_ref.at[0], wu_ref.at[0], wd_ref.at[0])


def _moe_grouped_ffn(tile_expert, n_used, xs, wg, wu, wd, *, tmg, tf):
    P, D = xs.shape
    E, _, F = wg.shape
    J = F // tf

    def f_block(g, j, nu):
        return jnp.where(g < nu[0], j, J - 1)

    grid_spec = pltpu.PrefetchScalarGridSpec(
        num_scalar_prefetch=2,
        grid=(P // tmg, J),
        in_specs=[
            pl.BlockSpec((tmg, D), lambda g, j, te, nu: (jnp.minimum(g, nu[0] - 1), 0)),
            pl.BlockSpec((1, D, tf), lambda g, j, te, nu: (te[g], 0, f_block(g, j, nu))),
            pl.BlockSpec((1, D, tf), lambda g, j, te, nu: (te[g], 0, f_block(g, j, nu))),
            pl.BlockSpec((1, tf, D), lambda g, j, te, nu: (te[g], f_block(g, j, nu), 0)),
        ],
        out_specs=pl.BlockSpec((tmg, D), lambda g, j, te, nu: (g, 0)),
        scratch_shapes=[pltpu.VMEM((tmg, D), BF16)],
    )
    return pl.pallas_call(
        _moe_group_kernel,
        grid_spec=grid_spec,
        out_shape=jax.ShapeDtypeStruct((P, D), F32),
        compiler_params=_params(("arbitrary", "arbitrary"), vmem=MOE_VMEM_LIMIT),
        name="moe_ffn",
    )(tile_expert, n_used, xs, wg, wu, wd)


def _combine_kernel(pos_ref, y_ref, h_ref, meta_ref, o_ref, buf_ref, sem):
    tt = h_ref.shape[0]

    def issue(r, carry):
        for s in range(2):
            pltpu.make_async_copy(y_ref.at[pl.ds(pos_ref[0, 0, 2 * r + s], 1)],
                                  buf_ref.at[s, pl.ds(r, 1)], sem).start(priority=s)
        return carry

    lax.fori_loop(0, tt, issue, 0, unroll=DMA_ISSUE_UNROLL)
    for s in range(2):
        pltpu.make_async_copy(y_ref.at[pl.ds(0, tt)], buf_ref.at[s], sem).wait()
    meta = meta_ref[...]
    o_ref[...] = (h_ref[...] + meta[:, META_GATE:META_GATE + 1] * buf_ref[0]
                  + meta[:, META_GATE + 1:META_GATE + 2] * buf_ref[1])


def _combine(pos3, y, h, meta):
    T, D = h.shape
    n_tiles, _, n_slots = pos3.shape
    tt = n_slots // 2
    return pl.pallas_call(
        _combine_kernel,
        grid=(n_tiles,),
        in_specs=[
            pl.BlockSpec((1, 1, n_slots), lambda i: (i, 0, 0), memory_space=pltpu.SMEM),
            pl.BlockSpec(memory_space=pl.ANY),
            pl.BlockSpec((tt, D), lambda i: (i, 0)),
            pl.BlockSpec((tt, LANES), lambda i: (i, 0)),
        ],
        out_specs=pl.BlockSpec((tt, D), lambda i: (i, 0)),
        out_shape=jax.ShapeDtypeStruct((T, D), F32),
        scratch_shapes=[pltpu.VMEM((2, tt, D), F32), pltpu.SemaphoreType.DMA(())],
        compiler_params=_params(("arbitrary",)),
        name="moe_combine",
    )(pos3, y, h, meta)


def _moe_routed(f, h, router_w, wg, wu, wd, *, tmg=720, tf=512, tt=256):
    T, D = h.shape
    E = wg.shape[0]
    n_tiles = 2 * T // tmg + E
    meta, counts = _router(f, router_w)
    counts = counts[0, :E].astype(jnp.int32)
    tiles_per_expert = (counts + tmg - 1) // tmg
    tile_end = jnp.cumsum(tiles_per_expert)
    row_start = (tile_end - tiles_per_expert) * tmg
    n_used = tile_end[-1:]
    tile_id = jnp.minimum(jnp.arange(n_tiles, dtype=jnp.int32), n_used - 1)
    tile_expert = jnp.sum(tile_id[:, None] >= tile_end[None, :], axis=1).astype(jnp.int32)
    expert = meta[:, META_EXPERT:META_EXPERT + 2].astype(jnp.int32)
    rank = meta[:, META_RANK:META_RANK + 2].astype(jnp.int32)
    pos3 = (row_start[expert] + rank).reshape(T // tt, 1, 2 * tt)
    xs = _dispatch(pos3, f, n_tiles * tmg)
    y = _moe_grouped_ffn(tile_expert, n_used.astype(jnp.int32), xs, wg, wu, wd, tmg=tmg, tf=tf)
    return _combine(pos3, y, h, meta)


def _alibi_slopes():
    n = HEADS_DILATED + HEADS_MOBA
    s = jnp.asarray(2.0 ** (-8.0 * np.arange(1, n + 1) / n), dtype=F32)
    return s[:HEADS_DILATED], s[HEADS_DILATED:]


def _head_norm_rows(q_gain, k_gain, D):
    a = HEADS_DILATED * HEAD_DIM
    b = HEADS_MOBA * HEAD_DIM
    rest = D - a - b

    def row(g):
        return jnp.concatenate([jnp.tile(g[0], HEADS_DILATED), jnp.tile(g[1], HEADS_MOBA), jnp.ones((rest,), F32)])

    flag = jnp.concatenate([jnp.ones((a + b,), F32), jnp.zeros((rest,), F32)])
    gain = jnp.concatenate([row(q_gain) * Q_PRESCALE, row(k_gain), jnp.ones((D,), F32)])
    flags = jnp.concatenate([flag, flag, jnp.zeros((D,), F32)])
    return gain.reshape(1, 3 * D), flags.reshape(1, 3 * D)


def kernel(x, attn_norm, w_in, q_gain, k_gain, out_gain, w_out, ffn_norm, dense_w_gate, dense_w_up, dense_w_down, moe_router, moe_w_gate, moe_w_up, moe_w_down):
    B, S, D = x.shape
    depth = w_in.shape[0]
    slopes_a, slopes_b = _alibi_slopes()
    h = x.reshape(B * S, D)
    for layer in range(depth):
        head_gain, head_flag = _head_norm_rows(q_gain[layer], k_gain[layer], D)
        qkv = _qkv_proj(h, attn_norm[layer], w_in[layer].astype(BF16), head_gain, head_flag)
        og = out_gain[layer].reshape(1, D)
        dense = layer % 2 == 0
        i = layer // 2
        wg, wu, wd = ((dense_w_gate[i], dense_w_up[i], dense_w_down[i]) if dense
                      else (moe_w_gate[i], moe_w_up[i], moe_w_down[i]))
        oa, wg_b = _dilated_attention(qkv, og, slopes_a, wg.reshape(-1, wg.shape[-1]), B, S)
        ob, wu_b = _moba_attention(qkv, og, slopes_b, wu.reshape(-1, wu.shape[-1]), B, S)
        oc, wd_b = _stick_attention(qkv, og, wd.reshape(-1, wd.shape[-1]), B, S)
        wg_b, wu_b, wd_b = wg_b.reshape(wg.shape), wu_b.reshape(wu.shape), wd_b.reshape(wd.shape)
        h, f = _out_proj(oa, ob, oc, w_out[layer].astype(BF16), h, ffn_norm[layer], BF16 if dense else F32)
        if dense:
            h = _ffn_dense(f, h, wg_b, wu_b, wd_b)
        else:
            h = _moe_routed(f, h, moe_router[i], wg_b, wu_b, wd_b)
    return h.reshape(B, S, D)
```

```python
import functools

import numpy as np
import jax
import jax.numpy as jnp
from jax import lax
from jax.experimental import pallas as pl
from jax.experimental.pallas import tpu as pltpu

HEAD_DIM = 128
N_HEADS = 16
HEADS_DILATED = 6
HEADS_MOBA = 5
HEADS_STICK = 5
DILATED_PATTERNS = ((128, 1), (512, 4), (2048, 16))
MOBA_BLOCK = 256
MOBA_TOPK = 3
N_EXPERTS = 8
EPS = 1e-6

LANES = 128
SUBLANES = 8
VMEM_LIMIT = 56 * 1024 * 1024
MOE_VMEM_LIMIT = 60 * 1024 * 1024
DMA_ISSUE_UNROLL = 8
ATTN_TILE = 256
NEG_INF = float("-inf")
LOG2E = 1.4426950408889634
Q_PRESCALE = HEAD_DIM ** -0.5 * LOG2E

F32 = jnp.float32
BF16 = jnp.bfloat16


def _params(sem, vmem=VMEM_LIMIT):
    return pltpu.CompilerParams(dimension_semantics=sem, vmem_limit_bytes=vmem)


def _dot(a, b):
    return jnp.dot(a, b, preferred_element_type=F32)


def _dot_nt(a, b):
    return lax.dot_general(a, b, (((1,), (1,)), ((), ())), preferred_element_type=F32)


def _rms(x, gain):
    ms = jnp.mean(x * x, axis=-1, keepdims=True)
    return x * lax.rsqrt(ms + EPS) * gain


def _split_bf16(x):
    hi = x.astype(BF16)
    lo = (x - hi.astype(F32)).astype(BF16)
    return hi, lo


def _qkv_kernel(x_ref, g_ref, w_ref, hg_ref, hf_ref, o_ref, a_ref):
    @pl.when(pl.program_id(1) == 0)
    def _():
        a_ref[...] = _rms(x_ref[...], g_ref[...]).astype(BF16)

    y = _dot(a_ref[...], w_ref[...])
    for c in range(y.shape[1] // HEAD_DIM):
        cols = slice(c * HEAD_DIM, (c + 1) * HEAD_DIM)
        blk = y[:, cols]
        ms = jnp.mean(blk * blk, axis=-1, keepdims=True)
        normed = blk * lax.rsqrt(ms + EPS)
        o_ref[:, cols] = (jnp.where(hf_ref[:, cols] > 0, normed, blk) * hg_ref[:, cols]).astype(o_ref.dtype)


def _qkv_proj(h, gain, w_bf16, head_gain, head_flag, *, tm=1024, tn=768):
    T, D = h.shape
    N = w_bf16.shape[1]
    return pl.pallas_call(
        _qkv_kernel,
        grid=(T // tm, N // tn),
        in_specs=[
            pl.BlockSpec((tm, D), lambda i, j: (i, 0)),
            pl.BlockSpec((1, D), lambda i, j: (0, 0)),
            pl.BlockSpec((D, tn), lambda i, j: (0, j)),
            pl.BlockSpec((1, tn), lambda i, j: (0, j)),
            pl.BlockSpec((1, tn), lambda i, j: (0, j)),
        ],
        out_specs=pl.BlockSpec((tm, tn), lambda i, j: (i, j)),
        out_shape=jax.ShapeDtypeStruct((T, N), BF16),
        scratch_shapes=[pltpu.VMEM((tm, D), BF16)],
        compiler_params=_params(("parallel", "arbitrary")),
        name="qkv_proj",
    )(h, gain.reshape(1, D), w_bf16, head_gain, head_flag)


def _attn_specs(S, n_q_tiles, head0, tq):
    q_spec = pl.BlockSpec((tq, HEAD_DIM), lambda b, h, i, *_: (b * n_q_tiles + i, head0 + h))
    k_spec = pl.BlockSpec((S, HEAD_DIM), lambda b, h, i, *_: (b, N_HEADS + head0 + h))
    v_spec = pl.BlockSpec((S, HEAD_DIM), lambda b, h, i, *_: (b, 2 * N_HEADS + head0 + h))
    g_spec = pl.BlockSpec((1, HEAD_DIM), lambda b, h, i, *_: (0, head0 + h))
    o_spec = pl.BlockSpec((tq, HEAD_DIM), lambda b, h, i, *_: (b * n_q_tiles + i, h))
    return q_spec, k_spec, v_spec, g_spec, o_spec


def _side_cast_specs(w2d, grid):
    rows, cols = w2d.shape
    steps = int(np.prod(grid))
    n_blocks = max(n for n in range(1, steps + 1) if rows % n == 0 and (rows // n) % 16 == 0)

    def index(*ids):
        step = ids[0]
        for size, idx in zip(grid[1:], ids[1:]):
            step = step * size + idx
        return jnp.minimum(step, n_blocks - 1), 0

    spec = pl.BlockSpec((rows // n_blocks, cols), index)
    return spec, spec, jax.ShapeDtypeStruct(w2d.shape, BF16)


def _side_cast(w_ref, wb_ref):
    wb_ref[...] = w_ref[...].astype(BF16)


def _key_rows(ref, first_block, n_blocks, tq):
    return ref[pl.ds(pl.multiple_of(first_block * tq, tq), n_blocks * tq), :]


def _lane_blocks(x, tq):
    return [x[:, w * tq:(w + 1) * tq] for w in range(x.shape[1] // tq)]


def _partial_softmax(scores, v_rows):
    m = jnp.max(functools.reduce(jnp.maximum, scores), axis=-1, keepdims=True)
    m_safe = jnp.where(m == NEG_INF, 0.0, m)
    p = [jnp.exp2(s - m_safe) for s in scores]
    l = jnp.sum(functools.reduce(jnp.add, p), axis=-1, keepdims=True)
    return m, l, _dot(jnp.concatenate([x.astype(BF16) for x in p], axis=1), v_rows)


def _merged_softmax(parts):
    m = functools.reduce(jnp.maximum, [part[0] for part in parts])
    weights = [jnp.exp2(part[0] - m) for part in parts]
    l = functools.reduce(jnp.add, [w * part[1] for w, part in zip(weights, parts)])
    acc = functools.reduce(jnp.add, [w * part[2] for w, part in zip(weights, parts)])
    return acc / l


def _dilated_kernel(q_ref, q_next_ref, k_ref, v_ref, g_ref, bias_ref, w_ref, o_ref, wb_ref,
                    s_even_ref, s_odd_ref, *, tq, n_win):
    i = pl.program_id(2)
    n_tiles = pl.num_programs(2)

    def raw_scores(tile, q):
        before = jnp.minimum(tile, n_win - 1)
        return _dot_nt(q, _key_rows(k_ref, tile - before, n_win, tq))

    @pl.when(i == 0)
    def _():
        s_even_ref[...] = raw_scores(0, q_ref[...])

    def step(s_ref, s_next_ref):
        _side_cast(w_ref, wb_ref)
        s_next_ref[...] = raw_scores(jnp.minimum(i + 1, n_tiles - 1), q_next_ref[...])
        before = jnp.minimum(i, n_win - 1)
        scores = [s_ref[:, w * tq:(w + 1) * tq] + bias_ref[0, before + n_win - 1 - w] for w in range(n_win)]
        part = _partial_softmax(scores, _key_rows(v_ref, i - before, n_win, tq))
        o_ref[...] = _rms(part[2] / part[1], g_ref[...]).astype(o_ref.dtype)

    @pl.when(i % 2 == 0)
    def _():
        step(s_even_ref, s_odd_ref)

    @pl.when(i % 2 == 1)
    def _():
        step(s_odd_ref, s_even_ref)


def _dilated_bias(slopes, tq, n_back, n_win):
    d = jnp.arange(n_back + n_win, dtype=jnp.int32)[:, None, None] - (n_win - 1)
    r = jnp.arange(tq, dtype=jnp.int32)[None, :, None]
    c = jnp.arange(tq, dtype=jnp.int32)[None, None, :]
    dist = d * tq + r - c
    mult = jnp.zeros(dist.shape, F32)
    for window, dilation in DILATED_PATTERNS:
        hit = (dist >= 0) & (dist <= window) & (dist % dilation == 0)
        mult = mult + hit.astype(F32)
    logm = jnp.where(mult > 0, jnp.log2(jnp.maximum(mult, 1.0)), NEG_INF)
    return logm[None] - (slopes * LOG2E)[:, None, None, None] * dist.astype(F32)[None]


def _dilated_attention(qkv, out_gain, slopes, w2d, B, S, *, tq=ATTN_TILE):
    nq = S // tq
    max_window = max(w for w, _ in DILATED_PATTERNS)
    n_back = -(-max_window // tq)
    n_win = min(n_back + 1, nq)
    bias = _dilated_bias(slopes, tq, n_back, n_win)
    grid = (B, HEADS_DILATED, nq)
    q_spec, k_spec, v_spec, g_spec, o_spec = _attn_specs(S, nq, 0, tq)
    q_next_spec = pl.BlockSpec((tq, HEAD_DIM), lambda b, h, i: (b * nq + jnp.minimum(i + 1, nq - 1), h))
    bias_spec = pl.BlockSpec((1, n_back + n_win, tq, tq), lambda b, h, i: (h, 0, 0, 0))
    w_spec, wb_spec, wb_shape = _side_cast_specs(w2d, grid)
    return pl.pallas_call(
        functools.partial(_dilated_kernel, tq=tq, n_win=n_win),
        grid=grid,
        in_specs=[q_spec, q_next_spec, k_spec, v_spec, g_spec, bias_spec, w_spec],
        out_specs=[o_spec, wb_spec],
        scratch_shapes=[pltpu.VMEM((tq, n_win * tq), F32)] * 2,
        out_shape=[jax.ShapeDtypeStruct((B * S, HEADS_DILATED * HEAD_DIM), BF16), wb_shape],
        compiler_params=_params(("arbitrary", "arbitrary", "arbitrary")),
        name="dilated_attn",
    )(qkv, qkv, qkv, qkv, out_gain, bias, w2d)


def _moba_block_bias(q, km_hi, km_lo, own, slope, kb):
    gate = _dot_nt(q, km_hi) + _dot_nt(q, km_lo)
    lane = lax.broadcasted_iota(jnp.int32, gate.shape, 1)
    lane_f = lane.astype(F32)
    g = jnp.where(lane < own, gate, NEG_INF)
    sel = jnp.zeros(gate.shape, jnp.bool_)
    for _ in range(MOBA_TOPK):
        top = jnp.max(g, axis=-1, keepdims=True)
        is_top = (g == top) & (top > NEG_INF)
        first = jnp.min(jnp.where(is_top, lane_f, float(LANES)), axis=-1, keepdims=True)
        pick = lane_f == first
        sel = sel | pick
        g = jnp.where(pick, NEG_INF, g)
    block_alibi = slope * ((own - lane) * kb).astype(F32)
    return jnp.where(lane < own, jnp.where(sel, 0.0, NEG_INF) - block_alibi,
                     jnp.where(lane == own, 0.0, NEG_INF))


def _moba_kernel(slopes_ref, q_ref, q_all_ref, k_ref, v_ref, g_ref, w_ref, o_ref, wb_ref,
                 kmean_ref, block_bias_ref, *, kb, n_blocks, group):
    h = pl.program_id(1)
    i = pl.program_id(2)
    slope = slopes_ref[h]

    @pl.when(i == 0)
    def _():
        kmean_ref[...] = jnp.zeros_like(kmean_ref)
        for n in range(n_blocks):
            rows = k_ref[n * kb:(n + 1) * kb, :].astype(F32)
            kmean_ref[n:n + 1, :] = jnp.mean(rows, axis=0, keepdims=True)
        km_hi, km_lo = _split_bf16(kmean_ref[...])
        for n in range(n_blocks):
            rows = slice(n * kb, (n + 1) * kb)
            block_bias_ref[rows, :] = _moba_block_bias(q_all_ref[rows, :], km_hi, km_lo, n, slope, kb)

    q = q_ref[...]
    per_block = block_bias_ref[pl.ds(pl.multiple_of(i * kb, kb), kb), :]
    rc = lax.broadcasted_iota(jnp.int32, (kb, kb), 0) - lax.broadcasted_iota(jnp.int32, (kb, kb), 1)
    in_block_alibi = slope * rc.astype(F32)
    own_group = i // group

    def attend(grp):
        rows = slice(grp * group * kb, (grp + 1) * group * kb)
        scores = []
        for w, s in enumerate(_lane_blocks(_dot_nt(q, k_ref[rows, :]), kb)):
            n = grp * group + w
            bias = jnp.broadcast_to(per_block[:, n:n + 1], (kb, kb)) - in_block_alibi
            scores.append(s + jnp.where((n == i) & (rc < 0), NEG_INF, bias))
        return _partial_softmax(scores, v_ref[rows, :])

    for last in range(n_blocks // group):
        @pl.when(own_group == last)
        def _():
            _side_cast(w_ref, wb_ref)
            out = _merged_softmax([attend(grp) for grp in range(last + 1)])
            o_ref[...] = _rms(out, g_ref[...]).astype(o_ref.dtype)


def _moba_attention(qkv, out_gain, slopes, w2d, B, S, *, kb=MOBA_BLOCK, group=4):
    n_blocks = S // kb
    assert S % kb == 0 and n_blocks <= LANES and n_blocks % group == 0
    head0 = HEADS_DILATED
    grid = (B, HEADS_MOBA, n_blocks)
    q_spec, k_spec, v_spec, g_spec, o_spec = _attn_specs(S, n_blocks, head0, kb)
    q_all_spec = pl.BlockSpec((S, HEAD_DIM), lambda b, h, i: (b, head0 + h))
    w_spec, wb_spec, wb_shape = _side_cast_specs(w2d, grid)
    return pl.pallas_call(
        functools.partial(_moba_kernel, kb=kb, n_blocks=n_blocks, group=group),
        grid=grid,
        in_specs=[pl.BlockSpec(memory_space=pltpu.SMEM), q_spec, q_all_spec, k_spec, v_spec, g_spec, w_spec],
        out_specs=[o_spec, wb_spec],
        scratch_shapes=[pltpu.VMEM((LANES, HEAD_DIM), F32), pltpu.VMEM((S, LANES), F32)],
        out_shape=[jax.ShapeDtypeStruct((B * S, HEADS_MOBA * HEAD_DIM), BF16), wb_shape],
        compiler_params=_params(("arbitrary", "arbitrary", "arbitrary")),
        name="moba_attn",
    )(slopes * LOG2E, qkv, qkv, qkv, qkv, out_gain, w2d)


def _stick_kernel(q_ref, k_ref, v_ref, g_ref, w_ref, o_ref, wb_ref, *, kb, group):
    _side_cast(w_ref, wb_ref)
    i = pl.program_id(2)
    q = q_ref[...]
    tq = q.shape[0]
    rc = lax.broadcasted_iota(jnp.int32, (tq, kb), 0) - lax.broadcasted_iota(jnp.int32, (tq, kb), 1)
    later = (lax.broadcasted_iota(jnp.int32, (kb, kb), 0)
             > lax.broadcasted_iota(jnp.int32, (kb, kb), 1)).astype(BF16)

    def attend(grp, carry, diagonal):
        z_blocks = _lane_blocks(_dot_nt(q, _key_rows(k_ref, grp * group, group, kb)), kb)
        weights = [None] * group
        for w in reversed(range(group)):
            z = z_blocks[w]
            log_1m = jnp.minimum(-z, 0.0) - jnp.log2(1.0 + jnp.exp2(-jnp.abs(z)))
            if diagonal:
                past = rc > w * kb
                log_1m = jnp.where(past, log_1m, 0.0)
            after = _dot(log_1m.astype(BF16), later) + carry
            a = jnp.exp2(z + log_1m + after)
            if diagonal:
                a = jnp.where(past, a, 0.0)
            weights[w] = a.astype(BF16)
            carry = carry + jnp.sum(log_1m, axis=-1, keepdims=True)
        return carry, _dot(jnp.concatenate(weights, axis=1), _key_rows(v_ref, grp * group, group, kb))

    carry, acc = attend(i, jnp.zeros((tq, 1), F32), True)

    odd = i % 2

    def single(state):
        carry, out = attend(i - 1, state[0], False)
        return carry, state[1] + out

    carry, acc = lax.cond(odd == 1, single, lambda state: state, (carry, acc))

    def body(t, state):
        first = i - 1 - odd - 2 * t
        carry, out_a = attend(first, state[0], False)
        carry, out_b = attend(first - 1, carry, False)
        return carry, state[1] + out_a + out_b

    _, acc = lax.fori_loop(0, i // 2, body, (carry, acc))
    o_ref[...] = _rms(acc, g_ref[...]).astype(o_ref.dtype)


def _stick_attention(qkv, out_gain, w2d, B, S, *, kb=ATTN_TILE, group=2):
    tq = kb * group
    nq = S // tq
    assert S % tq == 0
    head0 = HEADS_DILATED + HEADS_MOBA
    grid = (B, HEADS_STICK, nq)
    q_spec, k_spec, v_spec, g_spec, o_spec = _attn_specs(S, nq, head0, tq)
    w_spec, wb_spec, wb_shape = _side_cast_specs(w2d, grid)
    return pl.pallas_call(
        functools.partial(_stick_kernel, kb=kb, group=group),
        grid=grid,
        in_specs=[q_spec, k_spec, v_spec, g_spec, w_spec],
        out_specs=[o_spec, wb_spec],
        out_shape=[jax.ShapeDtypeStruct((B * S, HEADS_STICK * HEAD_DIM), BF16), wb_shape],
        compiler_params=_params(("arbitrary", "arbitrary", "arbitrary")),
        name="stick_attn",
    )(qkv, qkv, qkv, out_gain, w2d)


def _oproj_kernel(oa_ref, ob_ref, oc_ref, w_ref, h_ref, g_ref, hn_ref, f_ref):
    ka, kb = oa_ref.shape[1], ob_ref.shape[1]
    y = _dot(oa_ref[...], w_ref[0:ka, :])
    y = y + _dot(ob_ref[...], w_ref[ka:ka + kb, :])
    y = y + _dot(oc_ref[...], w_ref[ka + kb:, :])
    hn = h_ref[...] + y
    hn_ref[...] = hn
    f_ref[...] = _rms(hn, g_ref[...]).astype(f_ref.dtype)


def _out_proj(oa, ob, oc, w_bf16, h, gain, f_dtype, *, tm=512):
    T, D = h.shape
    row = lambda i: (i, 0)
    fixed = lambda i: (0, 0)
    return pl.pallas_call(
        _oproj_kernel,
        grid=(T // tm,),
        in_specs=[
            pl.BlockSpec((tm, oa.shape[1]), row),
            pl.BlockSpec((tm, ob.shape[1]), row),
            pl.BlockSpec((tm, oc.shape[1]), row),
            pl.BlockSpec((D, D), fixed),
            pl.BlockSpec((tm, D), row),
            pl.BlockSpec((1, D), fixed),
        ],
        out_specs=[pl.BlockSpec((tm, D), row), pl.BlockSpec((tm, D), row)],
        out_shape=[jax.ShapeDtypeStruct((T, D), F32), jax.ShapeDtypeStruct((T, D), f_dtype)],
        compiler_params=_params(("parallel",)),
        name="out_proj",
    )(oa, ob, oc, w_bf16, h, gain.reshape(1, D))


def _swiglu_block(f, wg_ref, wu_ref, wd_ref):
    g = _dot(f, wg_ref[...])
    u = _dot(f, wu_ref[...])
    a = (g * jax.nn.sigmoid(g) * u).astype(BF16)
    return _dot(a, wd_ref[...])


def _ffn_kernel(f_ref, wg_ref, wu_ref, wd_ref, h_ref, o_ref, acc_ref):
    j = pl.program_id(1)

    @pl.when(j == 0)
    def _():
        acc_ref[...] = h_ref[...]

    acc_ref[...] += _swiglu_block(f_ref[...], wg_ref, wu_ref, wd_ref)

    @pl.when(j == pl.num_programs(1) - 1)
    def _():
        o_ref[...] = acc_ref[...]


def _ffn_dense(f, h, wg, wu, wd, *, tm=512, tf=512):
    T, D = h.shape
    F = wg.shape[1]
    return pl.pallas_call(
        _ffn_kernel,
        grid=(T // tm, F // tf),
        in_specs=[
            pl.BlockSpec((tm, D), lambda i, j: (i, 0)),
            pl.BlockSpec((D, tf), lambda i, j: (0, j)),
            pl.BlockSpec((D, tf), lambda i, j: (0, j)),
            pl.BlockSpec((tf, D), lambda i, j: (j, 0)),
            pl.BlockSpec((tm, D), lambda i, j: (i, 0)),
        ],
        out_specs=pl.BlockSpec((tm, D), lambda i, j: (i, 0)),
        out_shape=jax.ShapeDtypeStruct((T, D), F32),
        scratch_shapes=[pltpu.VMEM((tm, D), F32)],
        compiler_params=_params(("parallel", "arbitrary")),
        name="ffn_dense",
    )(f, wg, wu, wd, h)


META_EXPERT, META_RANK, META_GATE = 0, 2, 4


def _router_kernel(f_ref, w_ref, meta_ref, count_ref):
    @pl.when(pl.program_id(0) == 0)
    def _():
        count_ref[...] = jnp.zeros_like(count_ref)

    logits = _dot(f_ref[...].astype(BF16), w_ref[...])
    tm = logits.shape[0]
    lane = lax.broadcasted_iota(jnp.int32, logits.shape, 1).astype(F32)
    g = jnp.where(lane < N_EXPERTS, logits, NEG_INF)
    picks, tops, experts = [], [], []
    for _ in range(2):
        top = jnp.max(g, axis=-1, keepdims=True)
        first = jnp.min(jnp.where(g == top, lane, float(LANES)), axis=-1, keepdims=True)
        pick = lane == first
        g = jnp.where(pick, NEG_INF, g)
        picks.append(pick)
        tops.append(top)
        experts.append(first)
    e2 = jnp.exp(tops[1] - tops[0])
    denom = 1.0 + e2
    gates = [1.0 / denom, e2 / denom]

    chosen = (picks[0] | picks[1]).astype(BF16)
    r = lax.broadcasted_iota(jnp.int32, (tm, tm), 0)
    c = lax.broadcasted_iota(jnp.int32, (tm, tm), 1)
    before = _dot((r > c).astype(BF16), chosen) + count_ref[...]
    ranks = [jnp.sum(jnp.where(p, before, 0.0), axis=-1, keepdims=True) for p in picks]
    count_ref[...] += jnp.sum(chosen.astype(F32), axis=0, keepdims=True)

    meta = jnp.zeros(logits.shape, F32)
    for base, pair in ((META_EXPERT, experts), (META_RANK, ranks), (META_GATE, gates)):
        for s in range(2):
            meta = jnp.where(lane == float(base + s), pair[s], meta)
    meta_ref[...] = meta


def _router(f, router_w, *, tm=512):
    T, D = f.shape
    w = jnp.zeros((D, LANES), BF16).at[:, :N_EXPERTS].set(router_w.astype(BF16))
    return pl.pallas_call(
        _router_kernel,
        grid=(T // tm,),
        in_specs=[pl.BlockSpec((tm, D), lambda i: (i, 0)), pl.BlockSpec((D, LANES), lambda i: (0, 0))],
        out_specs=[pl.BlockSpec((tm, LANES), lambda i: (i, 0)), pl.BlockSpec((1, LANES), lambda i: (0, 0))],
        out_shape=[jax.ShapeDtypeStruct((T, LANES), F32), jax.ShapeDtypeStruct((1, LANES), F32)],
        compiler_params=_params(("arbitrary",)),
        name="moe_router",
    )(f, w)


def _dispatch_kernel(pad_ref, pos_ref, f_ref, xs_ref, zero_ref, sem):
    tt = f_ref.shape[0]

    @pl.when(pl.program_id(0) == 0)
    def _():
        zero_ref[...] = jnp.zeros_like(zero_ref)
        n_experts = pad_ref.shape[0] - 1
        fills = [pltpu.make_async_copy(
            zero_ref, xs_ref.at[pl.ds(pl.multiple_of(pad_ref[e], SUBLANES), zero_ref.shape[0])], sem)
            for e in range(n_experts)]
        for fill in fills:
            fill.start()
        for fill in fills:
            fill.wait()

        tmg = zero_ref.shape[0] - SUBLANES

        def fill_unused(g, carry):
            fill = pltpu.make_async_copy(zero_ref.at[pl.ds(0, tmg)],
                                         xs_ref.at[pl.ds(pl.multiple_of(g * tmg, SUBLANES), tmg)], sem)
            fill.start()
            fill.wait()
            return carry

        lax.fori_loop(pad_ref[n_experts], xs_ref.shape[0] // tmg, fill_unused, 0)

    def issue(r, carry):
        src = f_ref.at[pl.ds(r, 1)]
        for s in range(2):
            pltpu.make_async_copy(src, xs_ref.at[pl.ds(pos_ref[0, 0, 2 * r + s], 1)], sem).start(priority=s)
        return carry

    lax.fori_loop(0, tt, issue, 0, unroll=DMA_ISSUE_UNROLL)
    for _ in range(2):
        pltpu.make_async_copy(f_ref, xs_ref.at[pl.ds(0, tt)], sem).wait()


def _dispatch(pad_start, pos3, f, n_rows, tmg):
    T, D = f.shape
    n_tiles, _, n_slots = pos3.shape
    return pl.pallas_call(
        _dispatch_kernel,
        grid=(n_tiles,),
        in_specs=[
            pl.BlockSpec(memory_space=pltpu.SMEM),
            pl.BlockSpec((1, 1, n_slots), lambda i: (i, 0, 0), memory_space=pltpu.SMEM),
            pl.BlockSpec((n_slots // 2, D), lambda i: (i, 0)),
        ],
        out_specs=pl.BlockSpec(memory_space=pl.ANY),
        out_shape=jax.ShapeDtypeStruct((n_rows, D), f.dtype),
        scratch_shapes=[pltpu.VMEM((tmg + SUBLANES, D), f.dtype), pltpu.SemaphoreType.DMA(())],
        compiler_params=_params(("arbitrary",)),
        name="moe_dispatch",
    )(pad_start, pos3, f)


def _moe_group_kernel(te_ref, nu_ref, x_ref, wg_ref, wu_ref, wd_ref, y_ref, xb_ref):
    del te_ref
    g = pl.program_id(0)
    j = pl.program_id(1)
    used = g < nu_ref[0]

    @pl.when(j == 0)
    def _():
        y_ref[...] = jnp.zeros_like(y_ref)
        xb_ref[...] = x_ref[...].astype(BF16)

    @pl.when(used)
    def _():
        y_ref[...] += _swiglu_block(xb_ref[...], wg_ref.at[0], wu_ref.at[0], wd_ref.at[0])


def _moe_grouped_ffn(tile_expert, n_used, xs, wg, wu, wd, *, tmg, tf):
    P, D = xs.shape
    E, _, F = wg.shape
    J = F // tf

    def f_block(g, j, nu):
        return jnp.where(g < nu[0], j, J - 1)

    grid_spec = pltpu.PrefetchScalarGridSpec(
        num_scalar_prefetch=2,
        grid=(P // tmg, J),
        in_specs=[
            pl.BlockSpec((tmg, D), lambda g, j, te, nu: (jnp.minimum(g, nu[0] - 1), 0)),
            pl.BlockSpec((1, D, tf), lambda g, j, te, nu: (te[g], 0, f_block(g, j, nu))),
            pl.BlockSpec((1, D, tf), lambda g, j, te, nu: (te[g], 0, f_block(g, j, nu))),
            pl.BlockSpec((1, tf, D), lambda g, j, te, nu: (te[g], f_block(g, j, nu), 0)),
        ],
        out_specs=pl.BlockSpec((tmg, D), lambda g, j, te, nu: (g, 0)),
        scratch_shapes=[pltpu.VMEM((tmg, D), BF16)],
    )
    return pl.pallas_call(
        _moe_group_kernel,
        grid_spec=grid_spec,
        out_shape=jax.ShapeDtypeStruct((P, D), F32),
        compiler_params=_params(("arbitrary", "arbitrary"), vmem=MOE_VMEM_LIMIT),
        name="moe_ffn",
    )(tile_expert, n_used, xs, wg, wu, wd)


def _combine_kernel(pos_ref, y_ref, h_ref, meta_ref, o_ref, buf_ref, sem):
    tt = h_ref.shape[0]

    def issue(r, carry):
        for s in range(2):
            pltpu.make_async_copy(y_ref.at[pl.ds(pos_ref[0, 0, 2 * r + s], 1)],
                                  buf_ref.at[s, pl.ds(r, 1)], sem).start(priority=s)
        return carry

    lax.fori_loop(0, tt, issue, 0, unroll=DMA_ISSUE_UNROLL)
    for s in range(2):
        pltpu.make_async_copy(y_ref.at[pl.ds(0, tt)], buf_ref.at[s], sem).wait()
    meta = meta_ref[...]
    o_ref[...] = (h_ref[...] + meta[:, META_GATE:META_GATE + 1] * buf_ref[0]
                  + meta[:, META_GATE + 1:META_GATE + 2] * buf_ref[1])


def _combine(pos3, y, h, meta):
    T, D = h.shape
    n_tiles, _, n_slots = pos3.shape
    tt = n_slots // 2
    return pl.pallas_call(
        _combine_kernel,
        grid=(n_tiles,),
        in_specs=[
            pl.BlockSpec((1, 1, n_slots), lambda i: (i, 0, 0), memory_space=pltpu.SMEM),
            pl.BlockSpec(memory_space=pl.ANY),
            pl.BlockSpec((tt, D), lambda i: (i, 0)),
            pl.BlockSpec((tt, LANES), lambda i: (i, 0)),
        ],
        out_specs=pl.BlockSpec((tt, D), lambda i: (i, 0)),
        out_shape=jax.ShapeDtypeStruct((T, D), F32),
        scratch_shapes=[pltpu.VMEM((2, tt, D), F32), pltpu.SemaphoreType.DMA(())],
        compiler_params=_params(("arbitrary",)),
        name="moe_combine",
    )(pos3, y, h, meta)


def _moe_routed(f, h, router_w, wg, wu, wd, *, tmg=720, tf=512, tt=256):
    T, D = h.shape
    E = wg.shape[0]
    n_tiles = 2 * T // tmg + E + 2
    meta, counts = _router(f, router_w)
    counts = counts[0, :E].astype(jnp.int32)
    tiles_per_expert = (counts + tmg - 1) // tmg
    tile_end = jnp.cumsum(tiles_per_expert)
    row_start = (tile_end - tiles_per_expert) * tmg
    n_used = tile_end[-1:]
    tile_id = jnp.minimum(jnp.arange(n_tiles, dtype=jnp.int32), n_used - 1)
    tile_expert = jnp.sum(tile_id[:, None] >= tile_end[None, :], axis=1).astype(jnp.int32)
    expert = meta[:, META_EXPERT:META_EXPERT + 2].astype(jnp.int32)
    rank = meta[:, META_RANK:META_RANK + 2].astype(jnp.int32)
    pos3 = (row_start[expert] + rank).reshape(T // tt, 1, 2 * tt)
    pad_start = jnp.concatenate([(row_start + counts) // SUBLANES * SUBLANES, n_used]).astype(jnp.int32)
    xs = _dispatch(pad_start, pos3, f, n_tiles * tmg, tmg)
    y = _moe_grouped_ffn(tile_expert, n_used.astype(jnp.int32), xs, wg, wu, wd, tmg=tmg, tf=tf)
    return _combine(pos3, y, h, meta)


def _alibi_slopes():
    n = HEADS_DILATED + HEADS_MOBA
    s = jnp.asarray(2.0 ** (-8.0 * np.arange(1, n + 1) / n), dtype=F32)
    return s[:HEADS_DILATED], s[HEADS_DILATED:]


def _head_norm_rows(q_gain, k_gain, D):
    a = HEADS_DILATED * HEAD_DIM
    b = HEADS_MOBA * HEAD_DIM
    rest = D - a - b

    def row(g):
        return jnp.concatenate([jnp.tile(g[0], HEADS_DILATED), jnp.tile(g[1], HEADS_MOBA), jnp.ones((rest,), F32)])

    flag = jnp.concatenate([jnp.ones((a + b,), F32), jnp.zeros((rest,), F32)])
    gain = jnp.concatenate([row(q_gain) * Q_PRESCALE, row(k_gain), jnp.ones((D,), F32)])
    flags = jnp.concatenate([flag, flag, jnp.zeros((D,), F32)])
    return gain.reshape(1, 3 * D), flags.reshape(1, 3 * D)


def kernel(x, attn_norm, w_in, q_gain, k_gain, out_gain, w_out, ffn_norm, dense_w_gate, dense_w_up, dense_w_down, moe_router, moe_w_gate, moe_w_up, moe_w_down):
    B, S, D = x.shape
    depth = w_in.shape[0]
    slopes_a, slopes_b = _alibi_slopes()
    h = x.reshape(B * S, D)
    for layer in range(depth):
        head_gain, head_flag = _head_norm_rows(q_gain[layer], k_gain[layer], D)
        qkv = _qkv_proj(h, attn_norm[layer], w_in[layer].astype(BF16), head_gain, head_flag)
        og = out_gain[layer].reshape(1, D)
        dense = layer % 2 == 0
        i = layer // 2
        wg, wu, wd = ((dense_w_gate[i], dense_w_up[i], dense_w_down[i]) if dense
                      else (moe_w_gate[i], moe_w_up[i], moe_w_down[i]))
        oa, wg_b = _dilated_attention(qkv, og, slopes_a, wg.reshape(-1, wg.shape[-1]), B, S)
        ob, wu_b = _moba_attention(qkv, og, slopes_b, wu.reshape(-1, wu.shape[-1]), B, S)
        oc, wd_b = _stick_attention(qkv, og, wd.reshape(-1, wd.shape[-1]), B, S)
        wg_b, wu_b, wd_b = wg_b.reshape(wg.shape), wu_b.reshape(wu.shape), wd_b.reshape(wd.shape)
        h, f = _out_proj(oa, ob, oc, w_out[layer].astype(BF16), h, ffn_norm[layer], BF16 if dense else F32)
        if dense:
            h = _ffn_dense(f, h, wg_b, wu_b, wd_b)
        else:
            h = _moe_routed(f, h, moe_router[i], wg_b, wu_b, wd_b)
    return h.reshape(B, S, D)
```

```python
import functools

import numpy as np
import jax
import jax.numpy as jnp
from jax import lax
from jax.experimental import pallas as pl
from jax.experimental.pallas import tpu as pltpu

HEAD_DIM = 128
N_HEADS = 16
HEADS_DILATED = 6
HEADS_MOBA = 5
HEADS_STICK = 5
DILATED_PATTERNS = ((128, 1), (512, 4), (2048, 16))
MOBA_BLOCK = 256
MOBA_TOPK = 3
N_EXPERTS = 8
EPS = 1e-6

LANES = 128
SUBLANES = 8
VMEM_LIMIT = 56 * 1024 * 1024
MOE_VMEM_LIMIT = 60 * 1024 * 1024
DMA_ISSUE_UNROLL = 8
ATTN_TILE = 256
NEG_INF = float("-inf")
LOG2E = 1.4426950408889634
Q_PRESCALE = HEAD_DIM ** -0.5 * LOG2E

F32 = jnp.float32
BF16 = jnp.bfloat16


def _params(sem, vmem=VMEM_LIMIT):
    return pltpu.CompilerParams(dimension_semantics=sem, vmem_limit_bytes=vmem)


def _dot(a, b):
    return jnp.dot(a, b, preferred_element_type=F32)


def _dot_nt(a, b):
    return lax.dot_general(a, b, (((1,), (1,)), ((), ())), preferred_element_type=F32)


def _rms(x, gain):
    ms = jnp.mean(x * x, axis=-1, keepdims=True)
    return x * lax.rsqrt(ms + EPS) * gain


def _split_bf16(x):
    hi = x.astype(BF16)
    lo = (x - hi.astype(F32)).astype(BF16)
    return hi, lo


def _qkv_kernel(x_ref, g_ref, w_ref, hg_ref, hf_ref, o_ref, a_ref):
    @pl.when(pl.program_id(1) == 0)
    def _():
        a_ref[...] = _rms(x_ref[...], g_ref[...]).astype(BF16)

    y = _dot(a_ref[...], w_ref[...])
    for c in range(y.shape[1] // HEAD_DIM):
        cols = slice(c * HEAD_DIM, (c + 1) * HEAD_DIM)
        blk = y[:, cols]
        ms = jnp.mean(blk * blk, axis=-1, keepdims=True)
        normed = blk * lax.rsqrt(ms + EPS)
        o_ref[:, cols] = (jnp.where(hf_ref[:, cols] > 0, normed, blk) * hg_ref[:, cols]).astype(o_ref.dtype)


def _qkv_proj(h, gain, w_bf16, head_gain, head_flag, *, tm=1024, tn=1536):
    T, D = h.shape
    N = w_bf16.shape[1]
    return pl.pallas_call(
        _qkv_kernel,
        grid=(T // tm, N // tn),
        in_specs=[
            pl.BlockSpec((tm, D), lambda i, j: (i, 0)),
            pl.BlockSpec((1, D), lambda i, j: (0, 0)),
            pl.BlockSpec((D, tn), lambda i, j: (0, j)),
            pl.BlockSpec((1, tn), lambda i, j: (0, j)),
            pl.BlockSpec((1, tn), lambda i, j: (0, j)),
        ],
        out_specs=pl.BlockSpec((tm, tn), lambda i, j: (i, j)),
        out_shape=jax.ShapeDtypeStruct((T, N), BF16),
        scratch_shapes=[pltpu.VMEM((tm, D), BF16)],
        compiler_params=_params(("parallel", "arbitrary")),
        name="qkv_proj",
    )(h, gain.reshape(1, D), w_bf16, head_gain, head_flag)


def _attn_specs(S, n_q_tiles, head0, tq):
    q_spec = pl.BlockSpec((tq, HEAD_DIM), lambda b, h, i, *_: (b * n_q_tiles + i, head0 + h))
    k_spec = pl.BlockSpec((S, HEAD_DIM), lambda b, h, i, *_: (b, N_HEADS + head0 + h))
    v_spec = pl.BlockSpec((S, HEAD_DIM), lambda b, h, i, *_: (b, 2 * N_HEADS + head0 + h))
    g_spec = pl.BlockSpec((1, HEAD_DIM), lambda b, h, i, *_: (0, head0 + h))
    o_spec = pl.BlockSpec((tq, HEAD_DIM), lambda b, h, i, *_: (b * n_q_tiles + i, h))
    return q_spec, k_spec, v_spec, g_spec, o_spec


def _side_cast_specs(w2d, grid):
    rows, cols = w2d.shape
    steps = int(np.prod(grid))
    n_blocks = max(n for n in range(1, steps + 1) if rows % n == 0 and (rows // n) % 16 == 0)

    def index(*ids):
        step = ids[0]
        for size, idx in zip(grid[1:], ids[1:]):
            step = step * size + idx
        return jnp.minimum(step, n_blocks - 1), 0

    spec = pl.BlockSpec((rows // n_blocks, cols), index)
    return spec, spec, jax.ShapeDtypeStruct(w2d.shape, BF16)


def _side_cast(w_ref, wb_ref):
    wb_ref[...] = w_ref[...].astype(BF16)


def _key_rows(ref, first_block, n_blocks, tq):
    return ref[pl.ds(pl.multiple_of(first_block * tq, tq), n_blocks * tq), :]


def _lane_blocks(x, tq):
    return [x[:, w * tq:(w + 1) * tq] for w in range(x.shape[1] // tq)]


def _partial_softmax(scores, v_rows):
    m = jnp.max(functools.reduce(jnp.maximum, scores), axis=-1, keepdims=True)
    m_safe = jnp.where(m == NEG_INF, 0.0, m)
    p = [jnp.exp2(s - m_safe) for s in scores]
    l = jnp.sum(functools.reduce(jnp.add, p), axis=-1, keepdims=True)
    return m, l, _dot(jnp.concatenate([x.astype(BF16) for x in p], axis=1), v_rows)


def _merged_softmax(parts):
    m = functools.reduce(jnp.maximum, [part[0] for part in parts])
    weights = [jnp.exp2(part[0] - m) for part in parts]
    l = functools.reduce(jnp.add, [w * part[1] for w, part in zip(weights, parts)])
    acc = functools.reduce(jnp.add, [w * part[2] for w, part in zip(weights, parts)])
    return acc / l


def _dilated_kernel(q_ref, k_ref, v_ref, g_ref, bias_ref, w_ref, o_ref, wb_ref, *, tq, n_win):
    _side_cast(w_ref, wb_ref)
    per_step = q_ref.shape[0] // tq
    for t in range(per_step):
        rows = slice(t * tq, (t + 1) * tq)
        tile = pl.program_id(2) * per_step + t
        before = jnp.minimum(tile, n_win - 1)
        s_all = _dot_nt(q_ref[rows, :], _key_rows(k_ref, tile - before, n_win, tq))
        scores = [s + bias_ref[0, before + n_win - 1 - w] for w, s in enumerate(_lane_blocks(s_all, tq))]
        part = _partial_softmax(scores, _key_rows(v_ref, tile - before, n_win, tq))
        o_ref[rows, :] = _rms(part[2] / part[1], g_ref[...]).astype(o_ref.dtype)


def _dilated_bias(slopes, tq, n_back, n_win):
    d = jnp.arange(n_back + n_win, dtype=jnp.int32)[:, None, None] - (n_win - 1)
    r = jnp.arange(tq, dtype=jnp.int32)[None, :, None]
    c = jnp.arange(tq, dtype=jnp.int32)[None, None, :]
    dist = d * tq + r - c
    mult = jnp.zeros(dist.shape, F32)
    for window, dilation in DILATED_PATTERNS:
        hit = (dist >= 0) & (dist <= window) & (dist % dilation == 0)
        mult = mult + hit.astype(F32)
    logm = jnp.where(mult > 0, jnp.log2(jnp.maximum(mult, 1.0)), NEG_INF)
    return logm[None] - (slopes * LOG2E)[:, None, None, None] * dist.astype(F32)[None]


def _dilated_attention(qkv, out_gain, slopes, w2d, B, S, *, tq=ATTN_TILE, tiles_per_step=2):
    nq = S // tq
    assert S % (tq * tiles_per_step) == 0
    max_window = max(w for w, _ in DILATED_PATTERNS)
    n_back = -(-max_window // tq)
    n_win = min(n_back + 1, nq)
    bias = _dilated_bias(slopes, tq, n_back, n_win)
    n_steps = nq // tiles_per_step
    grid = (B, HEADS_DILATED, n_steps)
    q_spec, k_spec, v_spec, g_spec, o_spec = _attn_specs(S, n_steps, 0, tq * tiles_per_step)
    bias_spec = pl.BlockSpec((1, n_back + n_win, tq, tq), lambda b, h, i: (h, 0, 0, 0))
    w_spec, wb_spec, wb_shape = _side_cast_specs(w2d, grid)
    return pl.pallas_call(
        functools.partial(_dilated_kernel, tq=tq, n_win=n_win),
        grid=grid,
        in_specs=[q_spec, k_spec, v_spec, g_spec, bias_spec, w_spec],
        out_specs=[o_spec, wb_spec],
        out_shape=[jax.ShapeDtypeStruct((B * S, HEADS_DILATED * HEAD_DIM), BF16), wb_shape],
        compiler_params=_params(("arbitrary", "arbitrary", "arbitrary")),
        name="dilated_attn",
    )(qkv, qkv, qkv, out_gain, bias, w2d)


def _moba_block_bias(q, km_hi, km_lo, own, slope, kb):
    gate = _dot_nt(q, km_hi) + _dot_nt(q, km_lo)
    lane = lax.broadcasted_iota(jnp.int32, gate.shape, 1)
    lane_f = lane.astype(F32)
    g = jnp.where(lane < own, gate, NEG_INF)
    sel = jnp.zeros(gate.shape, jnp.bool_)
    for _ in range(MOBA_TOPK):
        top = jnp.max(g, axis=-1, keepdims=True)
        is_top = (g == top) & (top > NEG_INF)
        first = jnp.min(jnp.where(is_top, lane_f, float(LANES)), axis=-1, keepdims=True)
        pick = lane_f == first
        sel = sel | pick
        g = jnp.where(pick, NEG_INF, g)
    block_alibi = slope * ((own - lane) * kb).astype(F32)
    return jnp.where(lane < own, jnp.where(sel, 0.0, NEG_INF) - block_alibi,
                     jnp.where(lane == own, 0.0, NEG_INF))


def _moba_kernel(slopes_ref, q_ref, q_all_ref, k_ref, v_ref, g_ref, w_ref, o_ref, wb_ref,
                 kmean_ref, block_bias_ref, *, kb, n_blocks, group):
    h = pl.program_id(1)
    i = pl.program_id(2)
    slope = slopes_ref[h]

    @pl.when(i == 0)
    def _():
        kmean_ref[...] = jnp.zeros_like(kmean_ref)
        for n in range(n_blocks):
            rows = k_ref[n * kb:(n + 1) * kb, :].astype(F32)
            kmean_ref[n:n + 1, :] = jnp.mean(rows, axis=0, keepdims=True)
        km_hi, km_lo = _split_bf16(kmean_ref[...])
        for n in range(n_blocks):
            rows = slice(n * kb, (n + 1) * kb)
            block_bias_ref[rows, :] = _moba_block_bias(q_all_ref[rows, :], km_hi, km_lo, n, slope, kb)

    per_step = q_ref.shape[0] // kb
    rc = lax.broadcasted_iota(jnp.int32, (kb, kb), 0) - lax.broadcasted_iota(jnp.int32, (kb, kb), 1)
    in_block_alibi = slope * rc.astype(F32)
    own_group = (i * per_step) // group

    def attend(t, grp):
        own = i * per_step + t
        q = q_ref[t * kb:(t + 1) * kb, :]
        per_block = block_bias_ref[pl.ds(pl.multiple_of(own * kb, kb), kb), :]
        rows = slice(grp * group * kb, (grp + 1) * group * kb)
        scores = []
        for w, s in enumerate(_lane_blocks(_dot_nt(q, k_ref[rows, :]), kb)):
            n = grp * group + w
            bias = jnp.broadcast_to(per_block[:, n:n + 1], (kb, kb)) - in_block_alibi
            scores.append(s + jnp.where((n == own) & (rc < 0), NEG_INF, bias))
        return _partial_softmax(scores, v_ref[rows, :])

    for last in range(n_blocks // group):
        @pl.when(own_group == last)
        def _():
            _side_cast(w_ref, wb_ref)
            for t in range(per_step):
                out = _merged_softmax([attend(t, grp) for grp in range(last + 1)])
                o_ref[t * kb:(t + 1) * kb, :] = _rms(out, g_ref[...]).astype(o_ref.dtype)


def _moba_attention(qkv, out_gain, slopes, w2d, B, S, *, kb=MOBA_BLOCK, group=4, blocks_per_step=2):
    n_blocks = S // kb
    assert S % kb == 0 and n_blocks <= LANES and n_blocks % group == 0 and group % blocks_per_step == 0
    head0 = HEADS_DILATED
    n_steps = n_blocks // blocks_per_step
    grid = (B, HEADS_MOBA, n_steps)
    q_spec, k_spec, v_spec, g_spec, o_spec = _attn_specs(S, n_steps, head0, kb * blocks_per_step)
    q_all_spec = pl.BlockSpec((S, HEAD_DIM), lambda b, h, i: (b, head0 + h))
    w_spec, wb_spec, wb_shape = _side_cast_specs(w2d, grid)
    return pl.pallas_call(
        functools.partial(_moba_kernel, kb=kb, n_blocks=n_blocks, group=group),
        grid=grid,
        in_specs=[pl.BlockSpec(memory_space=pltpu.SMEM), q_spec, q_all_spec, k_spec, v_spec, g_spec, w_spec],
        out_specs=[o_spec, wb_spec],
        scratch_shapes=[pltpu.VMEM((LANES, HEAD_DIM), F32), pltpu.VMEM((S, LANES), F32)],
        out_shape=[jax.ShapeDtypeStruct((B * S, HEADS_MOBA * HEAD_DIM), BF16), wb_shape],
        compiler_params=_params(("arbitrary", "arbitrary", "arbitrary")),
        name="moba_attn",
    )(slopes * LOG2E, qkv, qkv, qkv, qkv, out_gain, w2d)


def _stick_kernel(q_ref, k_ref, v_ref, g_ref, w_ref, o_ref, wb_ref, *, kb, group):
    _side_cast(w_ref, wb_ref)
    i = pl.program_id(2)
    q = q_ref[...]
    tq = q.shape[0]
    rc = lax.broadcasted_iota(jnp.int32, (tq, kb), 0) - lax.broadcasted_iota(jnp.int32, (tq, kb), 1)
    later = (lax.broadcasted_iota(jnp.int32, (kb, kb), 0)
             > lax.broadcasted_iota(jnp.int32, (kb, kb), 1)).astype(BF16)

    def attend(grp, carry, diagonal):
        z_blocks = _lane_blocks(_dot_nt(q, _key_rows(k_ref, grp * group, group, kb)), kb)
        weights = [None] * group
        for w in reversed(range(group)):
            z = z_blocks[w]
            log_1m = jnp.minimum(-z, 0.0) - jnp.log2(1.0 + jnp.exp2(-jnp.abs(z)))
            if diagonal:
                past = rc > w * kb
                log_1m = jnp.where(past, log_1m, 0.0)
            after = _dot(log_1m.astype(BF16), later) + carry
            a = jnp.exp2(z + log_1m + after)
            if diagonal:
                a = jnp.where(past, a, 0.0)
            weights[w] = a.astype(BF16)
            carry = carry + jnp.sum(log_1m, axis=-1, keepdims=True)
        return carry, _dot(jnp.concatenate(weights, axis=1), _key_rows(v_ref, grp * group, group, kb))

    carry, acc = attend(i, jnp.zeros((tq, 1), F32), True)

    odd = i % 2

    def single(state):
        carry, out = attend(i - 1, state[0], False)
        return carry, state[1] + out

    carry, acc = lax.cond(odd == 1, single, lambda state: state, (carry, acc))

    def body(t, state):
        first = i - 1 - odd - 2 * t
        carry, out_a = attend(first, state[0], False)
        carry, out_b = attend(first - 1, carry, False)
        return carry, state[1] + out_a + out_b

    _, acc = lax.fori_loop(0, i // 2, body, (carry, acc))
    o_ref[...] = _rms(acc, g_ref[...]).astype(o_ref.dtype)


def _stick_attention(qkv, out_gain, w2d, B, S, *, kb=ATTN_TILE, group=2):
    tq = kb * group
    nq = S // tq
    assert S % tq == 0
    head0 = HEADS_DILATED + HEADS_MOBA
    grid = (B, HEADS_STICK, nq)
    q_spec, k_spec, v_spec, g_spec, o_spec = _attn_specs(S, nq, head0, tq)
    w_spec, wb_spec, wb_shape = _side_cast_specs(w2d, grid)
    return pl.pallas_call(
        functools.partial(_stick_kernel, kb=kb, group=group),
        grid=grid,
        in_specs=[q_spec, k_spec, v_spec, g_spec, w_spec],
        out_specs=[o_spec, wb_spec],
        out_shape=[jax.ShapeDtypeStruct((B * S, HEADS_STICK * HEAD_DIM), BF16), wb_shape],
        compiler_params=_params(("arbitrary", "arbitrary", "arbitrary")),
        name="stick_attn",
    )(qkv, qkv, qkv, out_gain, w2d)


def _oproj_kernel(oa_ref, ob_ref, oc_ref, w_ref, h_ref, g_ref, hn_ref, f_ref):
    ka, kb = oa_ref.shape[1], ob_ref.shape[1]
    y = _dot(oa_ref[...], w_ref[0:ka, :])
    y = y + _dot(ob_ref[...], w_ref[ka:ka + kb, :])
    y = y + _dot(oc_ref[...], w_ref[ka + kb:, :])
    hn = h_ref[...] + y
    hn_ref[...] = hn
    f_ref[...] = _rms(hn, g_ref[...]).astype(f_ref.dtype)


def _out_proj(oa, ob, oc, w_bf16, h, gain, f_dtype, *, tm=512):
    T, D = h.shape
    row = lambda i: (i, 0)
    fixed = lambda i: (0, 0)
    return pl.pallas_call(
        _oproj_kernel,
        grid=(T // tm,),
        in_specs=[
            pl.BlockSpec((tm, oa.shape[1]), row),
            pl.BlockSpec((tm, ob.shape[1]), row),
            pl.BlockSpec((tm, oc.shape[1]), row),
            pl.BlockSpec((D, D), fixed),
            pl.BlockSpec((tm, D), row),
            pl.BlockSpec((1, D), fixed),
        ],
        out_specs=[pl.BlockSpec((tm, D), row), pl.BlockSpec((tm, D), row)],
        out_shape=[jax.ShapeDtypeStruct((T, D), F32), jax.ShapeDtypeStruct((T, D), f_dtype)],
        compiler_params=_params(("parallel",)),
        name="out_proj",
    )(oa, ob, oc, w_bf16, h, gain.reshape(1, D))


def _swiglu_block(f, wg_ref, wu_ref, wd_ref):
    g = _dot(f, wg_ref[...])
    u = _dot(f, wu_ref[...])
    a = (g * jax.nn.sigmoid(g) * u).astype(BF16)
    return _dot(a, wd_ref[...])


def _ffn_kernel(f_ref, wg_ref, wu_ref, wd_ref, h_ref, o_ref, acc_ref):
    j = pl.program_id(1)

    @pl.when(j == 0)
    def _():
        acc_ref[...] = h_ref[...]

    acc_ref[...] += _swiglu_block(f_ref[...], wg_ref, wu_ref, wd_ref)

    @pl.when(j == pl.num_programs(1) - 1)
    def _():
        o_ref[...] = acc_ref[...]


def _ffn_dense(f, h, wg, wu, wd, *, tm=512, tf=512):
    T, D = h.shape
    F = wg.shape[1]
    return pl.pallas_call(
        _ffn_kernel,
        grid=(T // tm, F // tf),
        in_specs=[
            pl.BlockSpec((tm, D), lambda i, j: (i, 0)),
            pl.BlockSpec((D, tf), lambda i, j: (0, j)),
            pl.BlockSpec((D, tf), lambda i, j: (0, j)),
            pl.BlockSpec((tf, D), lambda i, j: (j, 0)),
            pl.BlockSpec((tm, D), lambda i, j: (i, 0)),
        ],
        out_specs=pl.BlockSpec((tm, D), lambda i, j: (i, 0)),
        out_shape=jax.ShapeDtypeStruct((T, D), F32),
        scratch_shapes=[pltpu.VMEM((tm, D), F32)],
        compiler_params=_params(("parallel", "arbitrary")),
        name="ffn_dense",
    )(f, wg, wu, wd, h)


META_EXPERT, META_RANK, META_GATE = 0, 2, 4


def _router_kernel(f_ref, w_ref, meta_ref, count_ref):
    @pl.when(pl.program_id(0) == 0)
    def _():
        count_ref[...] = jnp.zeros_like(count_ref)

    logits = _dot(f_ref[...].astype(BF16), w_ref[...])
    tm = logits.shape[0]
    lane = lax.broadcasted_iota(jnp.int32, logits.shape, 1).astype(F32)
    g = jnp.where(lane < N_EXPERTS, logits, NEG_INF)
    picks, tops, experts = [], [], []
    for _ in range(2):
        top = jnp.max(g, axis=-1, keepdims=True)
        first = jnp.min(jnp.where(g == top, lane, float(LANES)), axis=-1, keepdims=True)
        pick = lane == first
        g = jnp.where(pick, NEG_INF, g)
        picks.append(pick)
        tops.append(top)
        experts.append(first)
    e2 = jnp.exp(tops[1] - tops[0])
    denom = 1.0 + e2
    gates = [1.0 / denom, e2 / denom]

    chosen = (picks[0] | picks[1]).astype(BF16)
    r = lax.broadcasted_iota(jnp.int32, (tm, tm), 0)
    c = lax.broadcasted_iota(jnp.int32, (tm, tm), 1)
    before = _dot((r > c).astype(BF16), chosen) + count_ref[...]
    ranks = [jnp.sum(jnp.where(p, before, 0.0), axis=-1, keepdims=True) for p in picks]
    count_ref[...] += jnp.sum(chosen.astype(F32), axis=0, keepdims=True)

    meta = jnp.zeros(logits.shape, F32)
    for base, pair in ((META_EXPERT, experts), (META_RANK, ranks), (META_GATE, gates)):
        for s in range(2):
            meta = jnp.where(lane == float(base + s), pair[s], meta)
    meta_ref[...] = meta


def _router(f, router_w, *, tm=512):
    T, D = f.shape
    w = jnp.zeros((D, LANES), BF16).at[:, :N_EXPERTS].set(router_w.astype(BF16))
    return pl.pallas_call(
        _router_kernel,
        grid=(T // tm,),
        in_specs=[pl.BlockSpec((tm, D), lambda i: (i, 0)), pl.BlockSpec((D, LANES), lambda i: (0, 0))],
        out_specs=[pl.BlockSpec((tm, LANES), lambda i: (i, 0)), pl.BlockSpec((1, LANES), lambda i: (0, 0))],
        out_shape=[jax.ShapeDtypeStruct((T, LANES), F32), jax.ShapeDtypeStruct((1, LANES), F32)],
        compiler_params=_params(("arbitrary",)),
        name="moe_router",
    )(f, w)


def _dispatch_kernel(pad_ref, pos_ref, f_ref, xs_ref, zero_ref, sem):
    tt = f_ref.shape[0]

    @pl.when(pl.program_id(0) == 0)
    def _():
        zero_ref[...] = jnp.zeros_like(zero_ref)
        n_experts = pad_ref.shape[0] - 1
        fills = [pltpu.make_async_copy(
            zero_ref, xs_ref.at[pl.ds(pl.multiple_of(pad_ref[e], SUBLANES), zero_ref.shape[0])], sem)
            for e in range(n_experts)]
        for fill in fills:
            fill.start()
        for fill in fills:
            fill.wait()

        tmg = zero_ref.shape[0] - SUBLANES

        def fill_unused(g, carry):
            fill = pltpu.make_async_copy(zero_ref.at[pl.ds(0, tmg)],
                                         xs_ref.at[pl.ds(pl.multiple_of(g * tmg, SUBLANES), tmg)], sem)
            fill.start()
            fill.wait()
            return carry

        lax.fori_loop(pad_ref[n_experts], xs_ref.shape[0] // tmg, fill_unused, 0)

    def issue(r, carry):
        src = f_ref.at[pl.ds(r, 1)]
        for s in range(2):
            pltpu.make_async_copy(src, xs_ref.at[pl.ds(pos_ref[0, 0, 2 * r + s], 1)], sem).start(priority=s)
        return carry

    lax.fori_loop(0, tt, issue, 0, unroll=DMA_ISSUE_UNROLL)
    for _ in range(2):
        pltpu.make_async_copy(f_ref, xs_ref.at[pl.ds(0, tt)], sem).wait()


def _dispatch(pad_start, pos3, f, n_rows, tmg):
    T, D = f.shape
    n_tiles, _, n_slots = pos3.shape
    return pl.pallas_call(
        _dispatch_kernel,
        grid=(n_tiles,),
        in_specs=[
            pl.BlockSpec(memory_space=pltpu.SMEM),
            pl.BlockSpec((1, 1, n_slots), lambda i: (i, 0, 0), memory_space=pltpu.SMEM),
            pl.BlockSpec((n_slots // 2, D), lambda i: (i, 0)),
        ],
        out_specs=pl.BlockSpec(memory_space=pl.ANY),
        out_shape=jax.ShapeDtypeStruct((n_rows, D), f.dtype),
        scratch_shapes=[pltpu.VMEM((tmg + SUBLANES, D), f.dtype), pltpu.SemaphoreType.DMA(())],
        compiler_params=_params(("arbitrary",)),
        name="moe_dispatch",
    )(pad_start, pos3, f)


def _moe_group_kernel(te_ref, nu_ref, x_ref, wg_ref, wu_ref, wd_ref, y_ref, xb_ref):
    del te_ref
    g = pl.program_id(0)
    j = pl.program_id(1)
    used = g < nu_ref[0]

    @pl.when(j == 0)
    def _():
        y_ref[...] = jnp.zeros_like(y_ref)
        xb_ref[...] = x_ref[...].astype(BF16)

    @pl.when(used)
    def _():
        y_ref[...] += _swiglu_block(xb_ref[...], wg_ref.at[0], wu_ref.at[0], wd_ref.at[0])


def _moe_grouped_ffn(tile_expert, n_used, xs, wg, wu, wd, *, tmg, tf):
    P, D = xs.shape
    E, _, F = wg.shape
    J = F // tf

    def f_block(g, j, nu):
        return jnp.where(g < nu[0], j, J - 1)

    grid_spec = pltpu.PrefetchScalarGridSpec(
        num_scalar_prefetch=2,
        grid=(P // tmg, J),
        in_specs=[
            pl.BlockSpec((tmg, D), lambda g, j, te, nu: (jnp.minimum(g, nu[0] - 1), 0)),
            pl.BlockSpec((1, D, tf), lambda g, j, te, nu: (te[g], 0, f_block(g, j, nu))),
            pl.BlockSpec((1, D, tf), lambda g, j, te, nu: (te[g], 0, f_block(g, j, nu))),
            pl.BlockSpec((1, tf, D), lambda g, j, te, nu: (te[g], f_block(g, j, nu), 0)),
        ],
        out_specs=pl.BlockSpec((tmg, D), lambda g, j, te, nu: (g, 0)),
        scratch_shapes=[pltpu.VMEM((tmg, D), BF16)],
    )
    return pl.pallas_call(
        _moe_group_kernel,
        grid_spec=grid_spec,
        out_shape=jax.ShapeDtypeStruct((P, D), F32),
        compiler_params=_params(("arbitrary", "arbitrary"), vmem=MOE_VMEM_LIMIT),
        name="moe_ffn",
    )(tile_expert, n_used, xs, wg, wu, wd)


def _combine_kernel(pos_ref, y_ref, h_ref, meta_ref, o_ref, buf_ref, sem):
    tt = h_ref.shape[0]

    def issue(r, carry):
        for s in range(2):
            pltpu.make_async_copy(y_ref.at[pl.ds(pos_ref[0, 0, 2 * r + s], 1)],
                                  buf_ref.at[s, pl.ds(r, 1)], sem).start(priority=s)
        return carry

    lax.fori_loop(0, tt, issue, 0, unroll=DMA_ISSUE_UNROLL)
    for s in range(2):
        pltpu.make_async_copy(y_ref.at[pl.ds(0, tt)], buf_ref.at[s], sem).wait()
    meta = meta_ref[...]
    o_ref[...] = (h_ref[...] + meta[:, META_GATE:META_GATE + 1] * buf_ref[0]
                  + meta[:, META_GATE + 1:META_GATE + 2] * buf_ref[1])


def _combine(pos3, y, h, meta):
    T, D = h.shape
    n_tiles, _, n_slots = pos3.shape
    tt = n_slots // 2
    return pl.pallas_call(
        _combine_kernel,
        grid=(n_tiles,),
        in_specs=[
            pl.BlockSpec((1, 1, n_slots), lambda i: (i, 0, 0), memory_space=pltpu.SMEM),
            pl.BlockSpec(memory_space=pl.ANY),
            pl.BlockSpec((tt, D), lambda i: (i, 0)),
            pl.BlockSpec((tt, LANES), lambda i: (i, 0)),
        ],
        out_specs=pl.BlockSpec((tt, D), lambda i: (i, 0)),
        out_shape=jax.ShapeDtypeStruct((T, D), F32),
        scratch_shapes=[pltpu.VMEM((2, tt, D), F32), pltpu.SemaphoreType.DMA(())],
        compiler_params=_params(("arbitrary",)),
        name="moe_combine",
    )(pos3, y, h, meta)


def _moe_routed(f, h, router_w, wg, wu, wd, *, tmg=720, tf=512, tt=256):
    T, D = h.shape
    E = wg.shape[0]
    n_tiles = 2 * T // tmg + E + 2
    meta, counts = _router(f, router_w)
    counts = counts[0, :E].astype(jnp.int32)
    tiles_per_expert = (counts + tmg - 1) // tmg
    tile_end = jnp.cumsum(tiles_per_expert)
    row_start = (tile_end - tiles_per_expert) * tmg
    n_used = tile_end[-1:]
    tile_id = jnp.minimum(jnp.arange(n_tiles, dtype=jnp.int32), n_used - 1)
    tile_expert = jnp.sum(tile_id[:, None] >= tile_end[None, :], axis=1).astype(jnp.int32)
    expert = meta[:, META_EXPERT:META_EXPERT + 2].astype(jnp.int32)
    rank = meta[:, META_RANK:META_RANK + 2].astype(jnp.int32)
    pos3 = (row_start[expert] + rank).reshape(T // tt, 1, 2 * tt)
    pad_start = jnp.concatenate([(row_start + counts) // SUBLANES * SUBLANES, n_used]).astype(jnp.int32)
    xs = _dispatch(pad_start, pos3, f, n_tiles * tmg, tmg)
    y = _moe_grouped_ffn(tile_expert, n_used.astype(jnp.int32), xs, wg, wu, wd, tmg=tmg, tf=tf)
    return _combine(pos3, y, h, meta)


def _alibi_slopes():
    n = HEADS_DILATED + HEADS_MOBA
    s = jnp.asarray(2.0 ** (-8.0 * np.arange(1, n + 1) / n), dtype=F32)
    return s[:HEADS_DILATED], s[HEADS_DILATED:]


def _head_norm_rows(q_gain, k_gain, D):
    a = HEADS_DILATED * HEAD_DIM
    b = HEADS_MOBA * HEAD_DIM
    rest = D - a - b

    def row(g):
        return jnp.concatenate([jnp.tile(g[0], HEADS_DILATED), jnp.tile(g[1], HEADS_MOBA), jnp.ones((rest,), F32)])

    flag = jnp.concatenate([jnp.ones((a + b,), F32), jnp.zeros((rest,), F32)])
    gain = jnp.concatenate([row(q_gain) * Q_PRESCALE, row(k_gain), jnp.ones((D,), F32)])
    flags = jnp.concatenate([flag, flag, jnp.zeros((D,), F32)])
    return gain.reshape(1, 3 * D), flags.reshape(1, 3 * D)


def kernel(x, attn_norm, w_in, q_gain, k_gain, out_gain, w_out, ffn_norm, dense_w_gate, dense_w_up, dense_w_down, moe_router, moe_w_gate, moe_w_up, moe_w_down):
    B, S, D = x.shape
    depth = w_in.shape[0]
    slopes_a, slopes_b = _alibi_slopes()
    h = x.reshape(B * S, D)
    for layer in range(depth):
        head_gain, head_flag = _head_norm_rows(q_gain[layer], k_gain[layer], D)
        qkv = _qkv_proj(h, attn_norm[layer], w_in[layer].astype(BF16), head_gain, head_flag)
        og = out_gain[layer].reshape(1, D)
        dense = layer % 2 == 0
        i = layer // 2
        wg, wu, wd = ((dense_w_gate[i], dense_w_up[i], dense_w_down[i]) if dense
                      else (moe_w_gate[i], moe_w_up[i], moe_w_down[i]))
        oa, wg_b = _dilated_attention(qkv, og, slopes_a, wg.reshape(-1, wg.shape[-1]), B, S)
        ob, wu_b = _moba_attention(qkv, og, slopes_b, wu.reshape(-1, wu.shape[-1]), B, S)
        oc, wd_b = _stick_attention(qkv, og, wd.reshape(-1, wd.shape[-1]), B, S)
        wg_b, wu_b, wd_b = wg_b.reshape(wg.shape), wu_b.reshape(wu.shape), wd_b.reshape(wd.shape)
        h, f = _out_proj(oa, ob, oc, w_out[layer].astype(BF16), h, ffn_norm[layer], BF16 if dense else F32)
        if dense:
            h = _ffn_dense(f, h, wg_b, wu_b, wd_b)
        else:
            h = _moe_routed(f, h, moe_router[i], wg_b, wu_b, wd_b)
    return h.reshape(B, S, D)
```

```python
import functools

import numpy as np
import jax
import jax.numpy as jnp
from jax import lax
from jax.experimental import pallas as pl
from jax.experimental.pallas import tpu as pltpu

HEAD_DIM = 128
N_HEADS = 16
HEADS_DILATED = 6
HEADS_MOBA = 5
HEADS_STICK = 5
DILATED_PATTERNS = ((128, 1), (512, 4), (2048, 16))
MOBA_BLOCK = 256
MOBA_TOPK = 3
N_EXPERTS = 8
EPS = 1e-6

LANES = 128
SUBLANES = 8
VMEM_LIMIT = 56 * 1024 * 1024
MOE_VMEM_LIMIT = 60 * 1024 * 1024
DMA_ISSUE_UNROLL = 8
ATTN_TILE = 256
NEG_INF = float("-inf")
LOG2E = 1.4426950408889634
Q_PRESCALE = HEAD_DIM ** -0.5 * LOG2E

F32 = jnp.float32
BF16 = jnp.bfloat16


def _params(sem, vmem=VMEM_LIMIT):
    return pltpu.CompilerParams(dimension_semantics=sem, vmem_limit_bytes=vmem)


def _dot(a, b):
    return jnp.dot(a, b, preferred_element_type=F32)


def _dot_nt(a, b):
    return lax.dot_general(a, b, (((1,), (1,)), ((), ())), preferred_element_type=F32)


def _rms(x, gain):
    ms = jnp.mean(x * x, axis=-1, keepdims=True)
    return x * lax.rsqrt(ms + EPS) * gain


def _split_bf16(x):
    hi = x.astype(BF16)
    lo = (x - hi.astype(F32)).astype(BF16)
    return hi, lo


def _qkv_kernel(x_ref, g_ref, w_ref, hg_ref, hf_ref, o_ref, a_ref):
    @pl.when(pl.program_id(1) == 0)
    def _():
        a_ref[...] = _rms(x_ref[...], g_ref[...]).astype(BF16)

    y = _dot(a_ref[...], w_ref[...])
    for c in range(y.shape[1] // HEAD_DIM):
        cols = slice(c * HEAD_DIM, (c + 1) * HEAD_DIM)
        blk = y[:, cols]
        ms = jnp.mean(blk * blk, axis=-1, keepdims=True)
        normed = blk * lax.rsqrt(ms + EPS)
        o_ref[:, cols] = (jnp.where(hf_ref[:, cols] > 0, normed, blk) * hg_ref[:, cols]).astype(o_ref.dtype)


def _qkv_proj(h, gain, w_bf16, head_gain, head_flag, *, tm=1024, tn=1536):
    T, D = h.shape
    N = w_bf16.shape[1]
    return pl.pallas_call(
        _qkv_kernel,
        grid=(T // tm, N // tn),
        in_specs=[
            pl.BlockSpec((tm, D), lambda i, j: (i, 0)),
            pl.BlockSpec((1, D), lambda i, j: (0, 0)),
            pl.BlockSpec((D, tn), lambda i, j: (0, j)),
            pl.BlockSpec((1, tn), lambda i, j: (0, j)),
            pl.BlockSpec((1, tn), lambda i, j: (0, j)),
        ],
        out_specs=pl.BlockSpec((tm, tn), lambda i, j: (i, j)),
        out_shape=jax.ShapeDtypeStruct((T, N), BF16),
        scratch_shapes=[pltpu.VMEM((tm, D), BF16)],
        compiler_params=_params(("parallel", "arbitrary")),
        name="qkv_proj",
    )(h, gain.reshape(1, D), w_bf16, head_gain, head_flag)


def _attn_specs(S, n_q_tiles, head0, tq):
    q_spec = pl.BlockSpec((tq, HEAD_DIM), lambda b, h, i, *_: (b * n_q_tiles + i, head0 + h))
    k_spec = pl.BlockSpec((S, HEAD_DIM), lambda b, h, i, *_: (b, N_HEADS + head0 + h))
    v_spec = pl.BlockSpec((S, HEAD_DIM), lambda b, h, i, *_: (b, 2 * N_HEADS + head0 + h))
    g_spec = pl.BlockSpec((1, HEAD_DIM), lambda b, h, i, *_: (0, head0 + h))
    o_spec = pl.BlockSpec((tq, HEAD_DIM), lambda b, h, i, *_: (b * n_q_tiles + i, h))
    return q_spec, k_spec, v_spec, g_spec, o_spec


def _side_cast_specs(w2d, grid):
    rows, cols = w2d.shape
    steps = int(np.prod(grid))
    n_blocks = max(n for n in range(1, steps + 1) if rows % n == 0 and (rows // n) % 16 == 0)

    def index(*ids):
        step = ids[0]
        for size, idx in zip(grid[1:], ids[1:]):
            step = step * size + idx
        return jnp.minimum(step, n_blocks - 1), 0

    spec = pl.BlockSpec((rows // n_blocks, cols), index)
    return spec, spec, jax.ShapeDtypeStruct(w2d.shape, BF16)


def _side_cast(w_ref, wb_ref):
    wb_ref[...] = w_ref[...].astype(BF16)


def _key_rows(ref, first_block, n_blocks, tq):
    return ref[pl.ds(pl.multiple_of(first_block * tq, tq), n_blocks * tq), :]


def _lane_blocks(x, tq):
    return [x[:, w * tq:(w + 1) * tq] for w in range(x.shape[1] // tq)]


def _partial_softmax(scores, v_rows):
    m = jnp.max(functools.reduce(jnp.maximum, scores), axis=-1, keepdims=True)
    m_safe = jnp.where(m == NEG_INF, 0.0, m)
    p = [jnp.exp2(s - m_safe) for s in scores]
    l = jnp.sum(functools.reduce(jnp.add, p), axis=-1, keepdims=True)
    return m, l, _dot(jnp.concatenate([x.astype(BF16) for x in p], axis=1), v_rows)


def _merged_softmax(parts):
    m = functools.reduce(jnp.maximum, [part[0] for part in parts])
    weights = [jnp.exp2(part[0] - m) for part in parts]
    l = functools.reduce(jnp.add, [w * part[1] for w, part in zip(weights, parts)])
    acc = functools.reduce(jnp.add, [w * part[2] for w, part in zip(weights, parts)])
    return acc / l


def _dilated_kernel(q_ref, k_ref, v_ref, g_ref, bias_ref, *rest, tq, n_win):
    o_ref = rest[-1] if len(rest) == 1 else rest[1]
    if len(rest) == 3:
        _side_cast(rest[0], rest[2])
    per_step = q_ref.shape[0] // tq
    for t in range(per_step):
        rows = slice(t * tq, (t + 1) * tq)
        tile = pl.program_id(2) * per_step + t
        before = jnp.minimum(tile, n_win - 1)
        s_all = _dot_nt(q_ref[rows, :], _key_rows(k_ref, tile - before, n_win, tq))
        scores = [s + bias_ref[0, before + n_win - 1 - w] for w, s in enumerate(_lane_blocks(s_all, tq))]
        part = _partial_softmax(scores, _key_rows(v_ref, tile - before, n_win, tq))
        o_ref[rows, :] = _rms(part[2] / part[1], g_ref[...]).astype(o_ref.dtype)


def _dilated_bias(slopes, tq, n_back, n_win):
    d = jnp.arange(n_back + n_win, dtype=jnp.int32)[:, None, None] - (n_win - 1)
    r = jnp.arange(tq, dtype=jnp.int32)[None, :, None]
    c = jnp.arange(tq, dtype=jnp.int32)[None, None, :]
    dist = d * tq + r - c
    mult = jnp.zeros(dist.shape, F32)
    for window, dilation in DILATED_PATTERNS:
        hit = (dist >= 0) & (dist <= window) & (dist % dilation == 0)
        mult = mult + hit.astype(F32)
    logm = jnp.where(mult > 0, jnp.log2(jnp.maximum(mult, 1.0)), NEG_INF)
    return logm[None] - (slopes * LOG2E)[:, None, None, None] * dist.astype(F32)[None]


def _dilated_attention(qkv, out_gain, slopes, w2d, B, S, *, tq=ATTN_TILE, tiles_per_step=2):
    nq = S // tq
    assert S % (tq * tiles_per_step) == 0
    max_window = max(w for w, _ in DILATED_PATTERNS)
    n_back = -(-max_window // tq)
    n_win = min(n_back + 1, nq)
    bias = _dilated_bias(slopes, tq, n_back, n_win)
    n_steps = nq // tiles_per_step
    grid = (B, HEADS_DILATED, n_steps)
    q_spec, k_spec, v_spec, g_spec, o_spec = _attn_specs(S, n_steps, 0, tq * tiles_per_step)
    bias_spec = pl.BlockSpec((1, n_back + n_win, tq, tq), lambda b, h, i: (h, 0, 0, 0))
    in_specs = [q_spec, k_spec, v_spec, g_spec, bias_spec]
    out_specs = [o_spec]
    out_shape = [jax.ShapeDtypeStruct((B * S, HEADS_DILATED * HEAD_DIM), BF16)]
    args = [qkv, qkv, qkv, out_gain, bias]
    if w2d is not None:
        w_spec, wb_spec, wb_shape = _side_cast_specs(w2d, grid)
        in_specs.append(w_spec)
        out_specs.append(wb_spec)
        out_shape.append(wb_shape)
        args.append(w2d)
    outs = pl.pallas_call(
        functools.partial(_dilated_kernel, tq=tq, n_win=n_win),
        grid=grid,
        in_specs=in_specs,
        out_specs=out_specs,
        out_shape=out_shape,
        compiler_params=_params(("arbitrary", "arbitrary", "arbitrary")),
        name="dilated_attn",
    )(*args)
    return outs[0], (outs[1] if w2d is not None else None)


def _moba_block_bias(q, km_hi, km_lo, own, slope, kb):
    gate = _dot_nt(q, km_hi) + _dot_nt(q, km_lo)
    lane = lax.broadcasted_iota(jnp.int32, gate.shape, 1)
    lane_f = lane.astype(F32)
    g = jnp.where(lane < own, gate, NEG_INF)
    sel = jnp.zeros(gate.shape, jnp.bool_)
    for _ in range(MOBA_TOPK):
        top = jnp.max(g, axis=-1, keepdims=True)
        is_top = (g == top) & (top > NEG_INF)
        first = jnp.min(jnp.where(is_top, lane_f, float(LANES)), axis=-1, keepdims=True)
        pick = lane_f == first
        sel = sel | pick
        g = jnp.where(pick, NEG_INF, g)
    block_alibi = slope * ((own - lane) * kb).astype(F32)
    return jnp.where(lane < own, jnp.where(sel, 0.0, NEG_INF) - block_alibi,
                     jnp.where(lane == own, 0.0, NEG_INF))


def _moba_kernel(slopes_ref, q_ref, q_all_ref, k_ref, v_ref, g_ref, w_ref, o_ref, wb_ref,
                 kmean_ref, block_bias_ref, *, kb, n_blocks, group):
    h = pl.program_id(1)
    i = pl.program_id(2)
    slope = slopes_ref[h]

    @pl.when(i == 0)
    def _():
        kmean_ref[...] = jnp.zeros_like(kmean_ref)
        for n in range(n_blocks):
            rows = k_ref[n * kb:(n + 1) * kb, :].astype(F32)
            kmean_ref[n:n + 1, :] = jnp.mean(rows, axis=0, keepdims=True)
        km_hi, km_lo = _split_bf16(kmean_ref[...])
        for n in range(n_blocks):
            rows = slice(n * kb, (n + 1) * kb)
            block_bias_ref[rows, :] = _moba_block_bias(q_all_ref[rows, :], km_hi, km_lo, n, slope, kb)

    per_step = q_ref.shape[0] // kb
    rc = lax.broadcasted_iota(jnp.int32, (kb, kb), 0) - lax.broadcasted_iota(jnp.int32, (kb, kb), 1)
    in_block_alibi = slope * rc.astype(F32)
    own_group = (i * per_step) // group

    def attend(t, grp):
        own = i * per_step + t
        q = q_ref[t * kb:(t + 1) * kb, :]
        per_block = block_bias_ref[pl.ds(pl.multiple_of(own * kb, kb), kb), :]
        rows = slice(grp * group * kb, (grp + 1) * group * kb)
        scores = []
        for w, s in enumerate(_lane_blocks(_dot_nt(q, k_ref[rows, :]), kb)):
            n = grp * group + w
            bias = jnp.broadcast_to(per_block[:, n:n + 1], (kb, kb)) - in_block_alibi
            scores.append(s + jnp.where((n == own) & (rc < 0), NEG_INF, bias))
        return _partial_softmax(scores, v_ref[rows, :])

    for last in range(n_blocks // group):
        @pl.when(own_group == last)
        def _():
            _side_cast(w_ref, wb_ref)
            for t in range(per_step):
                out = _merged_softmax([attend(t, grp) for grp in range(last + 1)])
                o_ref[t * kb:(t + 1) * kb, :] = _rms(out, g_ref[...]).astype(o_ref.dtype)


def _moba_attention(qkv, out_gain, slopes, w2d, B, S, *, kb=MOBA_BLOCK, group=4, blocks_per_step=2):
    n_blocks = S // kb
    assert S % kb == 0 and n_blocks <= LANES and n_blocks % group == 0 and group % blocks_per_step == 0
    head0 = HEADS_DILATED
    n_steps = n_blocks // blocks_per_step
    grid = (B, HEADS_MOBA, n_steps)
    q_spec, k_spec, v_spec, g_spec, o_spec = _attn_specs(S, n_steps, head0, kb * blocks_per_step)
    q_all_spec = pl.BlockSpec((S, HEAD_DIM), lambda b, h, i: (b, head0 + h))
    w_spec, wb_spec, wb_shape = _side_cast_specs(w2d, grid)
    return pl.pallas_call(
        functools.partial(_moba_kernel, kb=kb, n_blocks=n_blocks, group=group),
        grid=grid,
        in_specs=[pl.BlockSpec(memory_space=pltpu.SMEM), q_spec, q_all_spec, k_spec, v_spec, g_spec, w_spec],
        out_specs=[o_spec, wb_spec],
        scratch_shapes=[pltpu.VMEM((LANES, HEAD_DIM), F32), pltpu.VMEM((S, LANES), F32)],
        out_shape=[jax.ShapeDtypeStruct((B * S, HEADS_MOBA * HEAD_DIM), BF16), wb_shape],
        compiler_params=_params(("arbitrary", "arbitrary", "arbitrary")),
        name="moba_attn",
    )(slopes * LOG2E, qkv, qkv, qkv, qkv, out_gain, w2d)


def _stick_kernel(q_ref, k_ref, v_ref, g_ref, w_ref, o_ref, wb_ref, *, kb, group):
    _side_cast(w_ref, wb_ref)
    i = pl.program_id(2)
    q = q_ref[...]
    tq = q.shape[0]
    rc = lax.broadcasted_iota(jnp.int32, (tq, kb), 0) - lax.broadcasted_iota(jnp.int32, (tq, kb), 1)
    later = (lax.broadcasted_iota(jnp.int32, (kb, kb), 0)
             > lax.broadcasted_iota(jnp.int32, (kb, kb), 1)).astype(BF16)

    def attend(grp, carry, diagonal):
        z_blocks = _lane_blocks(_dot_nt(q, _key_rows(k_ref, grp * group, group, kb)), kb)
        weights = [None] * group
        for w in reversed(range(group)):
            z = z_blocks[w]
            log_1m = jnp.minimum(-z, 0.0) - jnp.log2(1.0 + jnp.exp2(-jnp.abs(z)))
            if diagonal:
                past = rc > w * kb
                log_1m = jnp.where(past, log_1m, 0.0)
            after = _dot(log_1m.astype(BF16), later) + carry
            a = jnp.exp2(z + log_1m + after)
            if diagonal:
                a = jnp.where(past, a, 0.0)
            weights[w] = a.astype(BF16)
            carry = carry + jnp.sum(log_1m, axis=-1, keepdims=True)
        return carry, _dot(jnp.concatenate(weights, axis=1), _key_rows(v_ref, grp * group, group, kb))

    carry, acc = attend(i, jnp.zeros((tq, 1), F32), True)

    odd = i % 2

    def single(state):
        carry, out = attend(i - 1, state[0], False)
        return carry, state[1] + out

    carry, acc = lax.cond(odd == 1, single, lambda state: state, (carry, acc))

    def body(t, state):
        first = i - 1 - odd - 2 * t
        carry, out_a = attend(first, state[0], False)
        carry, out_b = attend(first - 1, carry, False)
        return carry, state[1] + out_a + out_b

    _, acc = lax.fori_loop(0, i // 2, body, (carry, acc))
    o_ref[...] = _rms(acc, g_ref[...]).astype(o_ref.dtype)


def _stick_attention(qkv, out_gain, w2d, B, S, *, kb=ATTN_TILE, group=2):
    tq = kb * group
    nq = S // tq
    assert S % tq == 0
    head0 = HEADS_DILATED + HEADS_MOBA
    grid = (B, HEADS_STICK, nq)
    q_spec, k_spec, v_spec, g_spec, o_spec = _attn_specs(S, nq, head0, tq)
    w_spec, wb_spec, wb_shape = _side_cast_specs(w2d, grid)
    return pl.pallas_call(
        functools.partial(_stick_kernel, kb=kb, group=group),
        grid=grid,
        in_specs=[q_spec, k_spec, v_spec, g_spec, w_spec],
        out_specs=[o_spec, wb_spec],
        out_shape=[jax.ShapeDtypeStruct((B * S, HEADS_STICK * HEAD_DIM), BF16), wb_shape],
        compiler_params=_params(("arbitrary", "arbitrary", "arbitrary")),
        name="stick_attn",
    )(qkv, qkv, qkv, out_gain, w2d)


def _oproj_kernel(oa_ref, ob_ref, oc_ref, w_ref, h_ref, g_ref, hn_ref, f_ref):
    ka, kb = oa_ref.shape[1], ob_ref.shape[1]
    y = _dot(oa_ref[...], w_ref[0:ka, :])
    y = y + _dot(ob_ref[...], w_ref[ka:ka + kb, :])
    y = y + _dot(oc_ref[...], w_ref[ka + kb:, :])
    hn = h_ref[...] + y
    hn_ref[...] = hn
    f_ref[...] = _rms(hn, g_ref[...]).astype(f_ref.dtype)


def _out_proj(oa, ob, oc, w_bf16, h, gain, f_dtype, *, tm=512):
    T, D = h.shape
    row = lambda i: (i, 0)
    fixed = lambda i: (0, 0)
    return pl.pallas_call(
        _oproj_kernel,
        grid=(T // tm,),
        in_specs=[
            pl.BlockSpec((tm, oa.shape[1]), row),
            pl.BlockSpec((tm, ob.shape[1]), row),
            pl.BlockSpec((tm, oc.shape[1]), row),
            pl.BlockSpec((D, D), fixed),
            pl.BlockSpec((tm, D), row),
            pl.BlockSpec((1, D), fixed),
        ],
        out_specs=[pl.BlockSpec((tm, D), row), pl.BlockSpec((tm, D), row)],
        out_shape=[jax.ShapeDtypeStruct((T, D), F32), jax.ShapeDtypeStruct((T, D), f_dtype)],
        compiler_params=_params(("parallel",)),
        name="out_proj",
    )(oa, ob, oc, w_bf16, h, gain.reshape(1, D))


def _swiglu_block(f, wg_ref, wu_ref, wd_ref):
    g = _dot(f, wg_ref[...])
    u = _dot(f, wu_ref[...])
    a = (g * jax.nn.sigmoid(g) * u).astype(BF16)
    return _dot(a, wd_ref[...])


def _ffn_kernel(f_ref, wg_ref, wu_ref, wd_ref, h_ref, *rest):
    o_ref, acc_ref = rest[-3 if len(rest) == 4 else -2], rest[-1]
    j = pl.program_id(1)

    @pl.when(j == 0)
    def _():
        acc_ref[...] = h_ref[...]

    if len(rest) == 4:
        _side_cast(rest[0], rest[2])
    acc_ref[...] += _swiglu_block(f_ref[...], wg_ref, wu_ref, wd_ref)

    @pl.when(j == pl.num_programs(1) - 1)
    def _():
        o_ref[...] = acc_ref[...]


def _ffn_dense(f, h, wg, wu, wd, side_w2d=None, *, tm=512, tf=512):
    T, D = h.shape
    F = wg.shape[1]
    grid = (T // tm, F // tf)
    in_specs = [
        pl.BlockSpec((tm, D), lambda i, j: (i, 0)),
        pl.BlockSpec((D, tf), lambda i, j: (0, j)),
        pl.BlockSpec((D, tf), lambda i, j: (0, j)),
        pl.BlockSpec((tf, D), lambda i, j: (j, 0)),
        pl.BlockSpec((tm, D), lambda i, j: (i, 0)),
    ]
    out_specs = [pl.BlockSpec((tm, D), lambda i, j: (i, 0))]
    out_shape = [jax.ShapeDtypeStruct((T, D), F32)]
    args = [f, wg, wu, wd, h]
    if side_w2d is not None:
        w_spec, wb_spec, wb_shape = _side_cast_specs(side_w2d, grid)
        in_specs.append(w_spec)
        out_specs.append(wb_spec)
        out_shape.append(wb_shape)
        args.append(side_w2d)
    outs = pl.pallas_call(
        _ffn_kernel,
        grid=grid,
        in_specs=in_specs,
        out_specs=out_specs,
        out_shape=out_shape,
        scratch_shapes=[pltpu.VMEM((tm, D), F32)],
        compiler_params=_params(("arbitrary", "arbitrary")),
        name="ffn_dense",
    )(*args)
    return outs[0], (outs[1] if side_w2d is not None else None)


META_EXPERT, META_RANK, META_GATE = 0, 2, 4


def _router_kernel(f_ref, w_ref, meta_ref, count_ref):
    @pl.when(pl.program_id(0) == 0)
    def _():
        count_ref[...] = jnp.zeros_like(count_ref)

    logits = _dot(f_ref[...].astype(BF16), w_ref[...])
    tm = logits.shape[0]
    lane = lax.broadcasted_iota(jnp.int32, logits.shape, 1).astype(F32)
    g = jnp.where(lane < N_EXPERTS, logits, NEG_INF)
    picks, tops, experts = [], [], []
    for _ in range(2):
        top = jnp.max(g, axis=-1, keepdims=True)
        first = jnp.min(jnp.where(g == top, lane, float(LANES)), axis=-1, keepdims=True)
        pick = lane == first
        g = jnp.where(pick, NEG_INF, g)
        picks.append(pick)
        tops.append(top)
        experts.append(first)
    e2 = jnp.exp(tops[1] - tops[0])
    denom = 1.0 + e2
    gates = [1.0 / denom, e2 / denom]

    chosen = (picks[0] | picks[1]).astype(BF16)
    r = lax.broadcasted_iota(jnp.int32, (tm, tm), 0)
    c = lax.broadcasted_iota(jnp.int32, (tm, tm), 1)
    before = _dot((r > c).astype(BF16), chosen) + count_ref[...]
    ranks = [jnp.sum(jnp.where(p, before, 0.0), axis=-1, keepdims=True) for p in picks]
    count_ref[...] += jnp.sum(chosen.astype(F32), axis=0, keepdims=True)

    meta = jnp.zeros(logits.shape, F32)
    for base, pair in ((META_EXPERT, experts), (META_RANK, ranks), (META_GATE, gates)):
        for s in range(2):
            meta = jnp.where(lane == float(base + s), pair[s], meta)
    meta_ref[...] = meta


def _router(f, router_w, *, tm=512):
    T, D = f.shape
    w = jnp.zeros((D, LANES), BF16).at[:, :N_EXPERTS].set(router_w.astype(BF16))
    return pl.pallas_call(
        _router_kernel,
        grid=(T // tm,),
        in_specs=[pl.BlockSpec((tm, D), lambda i: (i, 0)), pl.BlockSpec((D, LANES), lambda i: (0, 0))],
        out_specs=[pl.BlockSpec((tm, LANES), lambda i: (i, 0)), pl.BlockSpec((1, LANES), lambda i: (0, 0))],
        out_shape=[jax.ShapeDtypeStruct((T, LANES), F32), jax.ShapeDtypeStruct((1, LANES), F32)],
        compiler_params=_params(("arbitrary",)),
        name="moe_router",
    )(f, w)


def _dispatch_kernel(pad_ref, pos_ref, f_ref, xs_ref, zero_ref, sem):
    tt = f_ref.shape[0]

    @pl.when(pl.program_id(0) == 0)
    def _():
        zero_ref[...] = jnp.zeros_like(zero_ref)
        n_experts = pad_ref.shape[0] - 1
        fills = [pltpu.make_async_copy(
            zero_ref, xs_ref.at[pl.ds(pl.multiple_of(pad_ref[e], SUBLANES), zero_ref.shape[0])], sem)
            for e in range(n_experts)]
        tmg = zero_ref.shape[0] - SUBLANES

        def unused_tile_fill(g):
            return pltpu.make_async_copy(zero_ref.at[pl.ds(0, tmg)],
                                         xs_ref.at[pl.ds(pl.multiple_of(g * tmg, SUBLANES), tmg)], sem)

        def start_fill(g, carry):
            unused_tile_fill(g).start()
            return carry

        def wait_fill(g, carry):
            unused_tile_fill(g).wait()
            return carry

        for fill in fills:
            fill.start()
            fill.wait()
        lax.fori_loop(pad_ref[n_experts], xs_ref.shape[0] // tmg, start_fill, 0)
        lax.fori_loop(pad_ref[n_experts], xs_ref.shape[0] // tmg, wait_fill, 0)

    def issue(r, carry):
        src = f_ref.at[pl.ds(r, 1)]
        for s in range(2):
            pltpu.make_async_copy(src, xs_ref.at[pl.ds(pos_ref[0, 0, 2 * r + s], 1)], sem).start(priority=s)
        return carry

    lax.fori_loop(0, tt, issue, 0, unroll=DMA_ISSUE_UNROLL)
    for _ in range(2):
        pltpu.make_async_copy(f_ref, xs_ref.at[pl.ds(0, tt)], sem).wait()


def _dispatch(pad_start, pos3, f, n_rows, tmg):
    T, D = f.shape
    n_tiles, _, n_slots = pos3.shape
    return pl.pallas_call(
        _dispatch_kernel,
        grid=(n_tiles,),
        in_specs=[
            pl.BlockSpec(memory_space=pltpu.SMEM),
            pl.BlockSpec((1, 1, n_slots), lambda i: (i, 0, 0), memory_space=pltpu.SMEM),
            pl.BlockSpec((n_slots // 2, D), lambda i: (i, 0)),
        ],
        out_specs=pl.BlockSpec(memory_space=pl.ANY),
        out_shape=jax.ShapeDtypeStruct((n_rows, D), f.dtype),
        scratch_shapes=[pltpu.VMEM((tmg + SUBLANES, D), f.dtype), pltpu.SemaphoreType.DMA(())],
        compiler_params=_params(("arbitrary",)),
        name="moe_dispatch",
    )(pad_start, pos3, f)


def _moe_group_kernel(te_ref, nu_ref, x_ref, wg_ref, wu_ref, wd_ref, y_ref, xb_ref):
    del te_ref
    g = pl.program_id(0)
    j = pl.program_id(1)
    used = g < nu_ref[0]

    @pl.when(j == 0)
    def _():
        y_ref[...] = jnp.zeros_like(y_ref)
        xb_ref[...] = x_ref[...].astype(BF16)

    @pl.when(used)
    def _():
        y_ref[...] += _swiglu_block(xb_ref[...], wg_ref.at[0], wu_ref.at[0], wd_ref.at[0])


def _moe_grouped_ffn(tile_expert, n_used, xs, wg, wu, wd, *, tmg, tf):
    P, D = xs.shape
    E, _, F = wg.shape
    J = F // tf

    def f_block(g, j, nu):
        return jnp.where(g < nu[0], j, J - 1)

    grid_spec = pltpu.PrefetchScalarGridSpec(
        num_scalar_prefetch=2,
        grid=(P // tmg, J),
        in_specs=[
            pl.BlockSpec((tmg, D), lambda g, j, te, nu: (jnp.minimum(g, nu[0] - 1), 0)),
            pl.BlockSpec((1, D, tf), lambda g, j, te, nu: (te[g], 0, f_block(g, j, nu))),
            pl.BlockSpec((1, D, tf), lambda g, j, te, nu: (te[g], 0, f_block(g, j, nu))),
            pl.BlockSpec((1, tf, D), lambda g, j, te, nu: (te[g], f_block(g, j, nu), 0)),
        ],
        out_specs=pl.BlockSpec((tmg, D), lambda g, j, te, nu: (g, 0)),
        scratch_shapes=[pltpu.VMEM((tmg, D), BF16)],
    )
    return pl.pallas_call(
        _moe_group_kernel,
        grid_spec=grid_spec,
        out_shape=jax.ShapeDtypeStruct((P, D), F32),
        compiler_params=_params(("arbitrary", "arbitrary"), vmem=MOE_VMEM_LIMIT),
        name="moe_ffn",
    )(tile_expert, n_used, xs, wg, wu, wd)


def _combine_kernel(pos_ref, y_ref, h_ref, meta_ref, o_ref, buf_ref, sem):
    tt = h_ref.shape[0]

    def issue(r, carry):
        for s in range(2):
            pltpu.make_async_copy(y_ref.at[pl.ds(pos_ref[0, 0, 2 * r + s], 1)],
                                  buf_ref.at[s, pl.ds(r, 1)], sem).start(priority=s)
        return carry

    lax.fori_loop(0, tt, issue, 0, unroll=DMA_ISSUE_UNROLL)
    for s in range(2):
        pltpu.make_async_copy(y_ref.at[pl.ds(0, tt)], buf_ref.at[s], sem).wait()
    meta = meta_ref[...]
    o_ref[...] = (h_ref[...] + meta[:, META_GATE:META_GATE + 1] * buf_ref[0]
                  + meta[:, META_GATE + 1:META_GATE + 2] * buf_ref[1])


def _combine(pos3, y, h, meta):
    T, D = h.shape
    n_tiles, _, n_slots = pos3.shape
    tt = n_slots // 2
    return pl.pallas_call(
        _combine_kernel,
        grid=(n_tiles,),
        in_specs=[
            pl.BlockSpec((1, 1, n_slots), lambda i: (i, 0, 0), memory_space=pltpu.SMEM),
            pl.BlockSpec(memory_space=pl.ANY),
            pl.BlockSpec((tt, D), lambda i: (i, 0)),
            pl.BlockSpec((tt, LANES), lambda i: (i, 0)),
        ],
        out_specs=pl.BlockSpec((tt, D), lambda i: (i, 0)),
        out_shape=jax.ShapeDtypeStruct((T, D), F32),
        scratch_shapes=[pltpu.VMEM((2, tt, D), F32), pltpu.SemaphoreType.DMA(())],
        compiler_params=_params(("arbitrary",)),
        name="moe_combine",
    )(pos3, y, h, meta)


def _moe_routed(f, h, router_w, wg, wu, wd, *, tmg=1056, tf=512, tt=256):
    T, D = h.shape
    E = wg.shape[0]
    n_tiles = 2 * T // tmg + E + 2
    meta, counts = _router(f, router_w)
    counts = counts[0, :E].astype(jnp.int32)
    tiles_per_expert = (counts + tmg - 1) // tmg
    tile_end = jnp.cumsum(tiles_per_expert)
    row_start = (tile_end - tiles_per_expert) * tmg
    n_used = tile_end[-1:]
    tile_id = jnp.minimum(jnp.arange(n_tiles, dtype=jnp.int32), n_used - 1)
    tile_expert = jnp.sum(tile_id[:, None] >= tile_end[None, :], axis=1).astype(jnp.int32)
    expert = meta[:, META_EXPERT:META_EXPERT + 2].astype(jnp.int32)
    rank = meta[:, META_RANK:META_RANK + 2].astype(jnp.int32)
    pos3 = (row_start[expert] + rank).reshape(T // tt, 1, 2 * tt)
    pad_start = jnp.concatenate([(row_start + counts) // SUBLANES * SUBLANES, n_used]).astype(jnp.int32)
    xs = _dispatch(pad_start, pos3, f, n_tiles * tmg, tmg)
    y = _moe_grouped_ffn(tile_expert, n_used.astype(jnp.int32), xs, wg, wu, wd, tmg=tmg, tf=tf)
    return _combine(pos3, y, h, meta)


def _alibi_slopes():
    n = HEADS_DILATED + HEADS_MOBA
    s = jnp.asarray(2.0 ** (-8.0 * np.arange(1, n + 1) / n), dtype=F32)
    return s[:HEADS_DILATED], s[HEADS_DILATED:]


def _head_norm_rows(q_gain, k_gain, D):
    a = HEADS_DILATED * HEAD_DIM
    b = HEADS_MOBA * HEAD_DIM
    rest = D - a - b

    def row(g):
        return jnp.concatenate([jnp.tile(g[0], HEADS_DILATED), jnp.tile(g[1], HEADS_MOBA), jnp.ones((rest,), F32)])

    flag = jnp.concatenate([jnp.ones((a + b,), F32), jnp.zeros((rest,), F32)])
    gain = jnp.concatenate([row(q_gain) * Q_PRESCALE, row(k_gain), jnp.ones((D,), F32)])
    flags = jnp.concatenate([flag, flag, jnp.zeros((D,), F32)])
    return gain.reshape(1, 3 * D), flags.reshape(1, 3 * D)


def kernel(x, attn_norm, w_in, q_gain, k_gain, out_gain, w_out, ffn_norm, dense_w_gate, dense_w_up, dense_w_down, moe_router, moe_w_gate, moe_w_up, moe_w_down):
    B, S, D = x.shape
    depth = w_in.shape[0]
    slopes_a, slopes_b = _alibi_slopes()
    h = x.reshape(B * S, D)
    gate_cast_early = None
    for layer in range(depth):
        head_gain, head_flag = _head_norm_rows(q_gain[layer], k_gain[layer], D)
        qkv = _qkv_proj(h, attn_norm[layer], w_in[layer].astype(BF16), head_gain, head_flag)
        og = out_gain[layer].reshape(1, D)
        dense = layer % 2 == 0
        i = layer // 2
        wg, wu, wd = ((dense_w_gate[i], dense_w_up[i], dense_w_down[i]) if dense
                      else (moe_w_gate[i], moe_w_up[i], moe_w_down[i]))
        flat = lambda w: w.reshape(-1, w.shape[-1])
        oa, wg_b = _dilated_attention(qkv, og, slopes_a, None if gate_cast_early is not None else flat(wg), B, S)
        if gate_cast_early is not None:
            wg_b, gate_cast_early = gate_cast_early, None
        ob, wu_b = _moba_attention(qkv, og, slopes_b, flat(wu), B, S)
        oc, wd_b = _stick_attention(qkv, og, flat(wd), B, S)
        wg_b, wu_b, wd_b = wg_b.reshape(wg.shape), wu_b.reshape(wu.shape), wd_b.reshape(wd.shape)
        h, f = _out_proj(oa, ob, oc, w_out[layer].astype(BF16), h, ffn_norm[layer], BF16 if dense else F32)
        if dense:
            next_is_expert = layer + 1 < depth and (layer + 1) % 2 == 1
            h, gate_cast_early = _ffn_dense(f, h, wg_b, wu_b, wd_b,
                                            flat(moe_w_gate[(layer + 1) // 2]) if next_is_expert else None)
        else:
            h = _moe_routed(f, h, moe_router[i], wg_b, wu_b, wd_b)
    return h.reshape(B, S, D)
```

```python
import functools

import numpy as np
import jax
import jax.numpy as jnp
from jax import lax
from jax.experimental import pallas as pl
from jax.experimental.pallas import tpu as pltpu

HEAD_DIM = 128
N_HEADS = 16
HEADS_DILATED = 6
HEADS_MOBA = 5
HEADS_STICK = 5
DILATED_PATTERNS = ((128, 1), (512, 4), (2048, 16))
MOBA_BLOCK = 256
MOBA_TOPK = 3
N_EXPERTS = 8
EPS = 1e-6

LANES = 128
SUBLANES = 8
VMEM_LIMIT = 56 * 1024 * 1024
MOE_VMEM_LIMIT = 60 * 1024 * 1024
DMA_ISSUE_UNROLL = 8
ATTN_TILE = 256
NEG_INF = float("-inf")
LOG2E = 1.4426950408889634
Q_PRESCALE = HEAD_DIM ** -0.5 * LOG2E

F32 = jnp.float32
BF16 = jnp.bfloat16


def _params(sem, vmem=VMEM_LIMIT):
    return pltpu.CompilerParams(dimension_semantics=sem, vmem_limit_bytes=vmem)


def _dot(a, b):
    return jnp.dot(a, b, preferred_element_type=F32)


def _dot_nt(a, b):
    return lax.dot_general(a, b, (((1,), (1,)), ((), ())), preferred_element_type=F32)


def _rms(x, gain):
    ms = jnp.mean(x * x, axis=-1, keepdims=True)
    return x * lax.rsqrt(ms + EPS) * gain


def _split_bf16(x):
    hi = x.astype(BF16)
    lo = (x - hi.astype(F32)).astype(BF16)
    return hi, lo


def _qkv_kernel(x_ref, g_ref, w_ref, hg_ref, hf_ref, o_ref, a_ref):
    @pl.when(pl.program_id(1) == 0)
    def _():
        a_ref[...] = _rms(x_ref[...], g_ref[...]).astype(BF16)

    y = _dot(a_ref[...], w_ref[...])
    for c in range(y.shape[1] // HEAD_DIM):
        cols = slice(c * HEAD_DIM, (c + 1) * HEAD_DIM)
        blk = y[:, cols]
        ms = jnp.mean(blk * blk, axis=-1, keepdims=True)
        normed = blk * lax.rsqrt(ms + EPS)
        o_ref[:, cols] = (jnp.where(hf_ref[:, cols] > 0, normed, blk) * hg_ref[:, cols]).astype(o_ref.dtype)


def _qkv_proj(h, gain, w_bf16, head_gain, head_flag, *, tm=1024, tn=1536):
    T, D = h.shape
    N = w_bf16.shape[1]
    return pl.pallas_call(
        _qkv_kernel,
        grid=(T // tm, N // tn),
        in_specs=[
            pl.BlockSpec((tm, D), lambda i, j: (i, 0)),
            pl.BlockSpec((1, D), lambda i, j: (0, 0)),
            pl.BlockSpec((D, tn), lambda i, j: (0, j)),
            pl.BlockSpec((1, tn), lambda i, j: (0, j)),
            pl.BlockSpec((1, tn), lambda i, j: (0, j)),
        ],
        out_specs=pl.BlockSpec((tm, tn), lambda i, j: (i, j)),
        out_shape=jax.ShapeDtypeStruct((T, N), BF16),
        scratch_shapes=[pltpu.VMEM((tm, D), BF16)],
        compiler_params=_params(("parallel", "arbitrary")),
        name="qkv_proj",
    )(h, gain.reshape(1, D), w_bf16, head_gain, head_flag)


def _attn_specs(S, n_q_tiles, head0, tq):
    q_spec = pl.BlockSpec((tq, HEAD_DIM), lambda b, h, i, *_: (b * n_q_tiles + i, head0 + h))
    k_spec = pl.BlockSpec((S, HEAD_DIM), lambda b, h, i, *_: (b, N_HEADS + head0 + h))
    v_spec = pl.BlockSpec((S, HEAD_DIM), lambda b, h, i, *_: (b, 2 * N_HEADS + head0 + h))
    g_spec = pl.BlockSpec((1, HEAD_DIM), lambda b, h, i, *_: (0, head0 + h))
    o_spec = pl.BlockSpec((tq, HEAD_DIM), lambda b, h, i, *_: (b * n_q_tiles + i, h))
    return q_spec, k_spec, v_spec, g_spec, o_spec


def _side_cast_specs(w2d, grid):
    rows, cols = w2d.shape
    steps = int(np.prod(grid))
    n_blocks = max(n for n in range(1, steps + 1) if rows % n == 0 and (rows // n) % 16 == 0)

    def index(*ids):
        step = ids[0]
        for size, idx in zip(grid[1:], ids[1:]):
            step = step * size + idx
        return jnp.minimum(step, n_blocks - 1), 0

    spec = pl.BlockSpec((rows // n_blocks, cols), index)
    return spec, spec, jax.ShapeDtypeStruct(w2d.shape, BF16)


def _side_cast(w_ref, wb_ref):
    wb_ref[...] = w_ref[...].astype(BF16)


def _key_rows(ref, first_block, n_blocks, tq):
    return ref[pl.ds(pl.multiple_of(first_block * tq, tq), n_blocks * tq), :]


def _lane_blocks(x, tq):
    return [x[:, w * tq:(w + 1) * tq] for w in range(x.shape[1] // tq)]


def _partial_softmax(scores, v_rows):
    m = jnp.max(functools.reduce(jnp.maximum, scores), axis=-1, keepdims=True)
    m_safe = jnp.where(m == NEG_INF, 0.0, m)
    p = [jnp.exp2(s - m_safe) for s in scores]
    l = jnp.sum(functools.reduce(jnp.add, p), axis=-1, keepdims=True)
    return m, l, _dot(jnp.concatenate([x.astype(BF16) for x in p], axis=1), v_rows)


def _merged_softmax(parts):
    m = functools.reduce(jnp.maximum, [part[0] for part in parts])
    weights = [jnp.exp2(part[0] - m) for part in parts]
    l = functools.reduce(jnp.add, [w * part[1] for w, part in zip(weights, parts)])
    acc = functools.reduce(jnp.add, [w * part[2] for w, part in zip(weights, parts)])
    return acc / l


def _dilated_kernel(q_ref, k_ref, v_ref, g_ref, bias_ref, w_ref, o_ref, wb_ref, *, tq, n_win):
    _side_cast(w_ref, wb_ref)
    per_step = q_ref.shape[0] // tq
    for t in range(per_step):
        rows = slice(t * tq, (t + 1) * tq)
        tile = pl.program_id(2) * per_step + t
        before = jnp.minimum(tile, n_win - 1)
        s_all = _dot_nt(q_ref[rows, :], _key_rows(k_ref, tile - before, n_win, tq))
        scores = [s + bias_ref[0, before + n_win - 1 - w] for w, s in enumerate(_lane_blocks(s_all, tq))]
        part = _partial_softmax(scores, _key_rows(v_ref, tile - before, n_win, tq))
        o_ref[rows, :] = _rms(part[2] / part[1], g_ref[...]).astype(o_ref.dtype)


def _dilated_bias(slopes, tq, n_back, n_win):
    d = jnp.arange(n_back + n_win, dtype=jnp.int32)[:, None, None] - (n_win - 1)
    r = jnp.arange(tq, dtype=jnp.int32)[None, :, None]
    c = jnp.arange(tq, dtype=jnp.int32)[None, None, :]
    dist = d * tq + r - c
    mult = jnp.zeros(dist.shape, F32)
    for window, dilation in DILATED_PATTERNS:
        hit = (dist >= 0) & (dist <= window) & (dist % dilation == 0)
        mult = mult + hit.astype(F32)
    logm = jnp.where(mult > 0, jnp.log2(jnp.maximum(mult, 1.0)), NEG_INF)
    return logm[None] - (slopes * LOG2E)[:, None, None, None] * dist.astype(F32)[None]


def _dilated_attention(qkv, out_gain, slopes, w2d, B, S, *, tq=ATTN_TILE, tiles_per_step=2):
    nq = S // tq
    assert S % (tq * tiles_per_step) == 0
    max_window = max(w for w, _ in DILATED_PATTERNS)
    n_back = -(-max_window // tq)
    n_win = min(n_back + 1, nq)
    bias = _dilated_bias(slopes, tq, n_back, n_win)
    n_steps = nq // tiles_per_step
    grid = (B, HEADS_DILATED, n_steps)
    q_spec, k_spec, v_spec, g_spec, o_spec = _attn_specs(S, n_steps, 0, tq * tiles_per_step)
    bias_spec = pl.BlockSpec((1, n_back + n_win, tq, tq), lambda b, h, i: (h, 0, 0, 0))
    w_spec, wb_spec, wb_shape = _side_cast_specs(w2d, grid)
    return pl.pallas_call(
        functools.partial(_dilated_kernel, tq=tq, n_win=n_win),
        grid=grid,
        in_specs=[q_spec, k_spec, v_spec, g_spec, bias_spec, w_spec],
        out_specs=[o_spec, wb_spec],
        out_shape=[jax.ShapeDtypeStruct((B * S, HEADS_DILATED * HEAD_DIM), BF16), wb_shape],
        compiler_params=_params(("arbitrary", "arbitrary", "arbitrary")),
        name="dilated_attn",
    )(qkv, qkv, qkv, out_gain, bias, w2d)


def _moba_block_bias(q, km_hi, km_lo, own, slope, kb):
    gate = _dot_nt(q, km_hi) + _dot_nt(q, km_lo)
    lane = lax.broadcasted_iota(jnp.int32, gate.shape, 1)
    lane_f = lane.astype(F32)
    g = jnp.where(lane < own, gate, NEG_INF)
    sel = jnp.zeros(gate.shape, jnp.bool_)
    for _ in range(MOBA_TOPK):
        top = jnp.max(g, axis=-1, keepdims=True)
        is_top = (g == top) & (top > NEG_INF)
        first = jnp.min(jnp.where(is_top, lane_f, float(LANES)), axis=-1, keepdims=True)
        pick = lane_f == first
        sel = sel | pick
        g = jnp.where(pick, NEG_INF, g)
    block_alibi = slope * ((own - lane) * kb).astype(F32)
    return jnp.where(lane < own, jnp.where(sel, 0.0, NEG_INF) - block_alibi,
                     jnp.where(lane == own, 0.0, NEG_INF))


def _moba_kernel(slopes_ref, q_ref, q_all_ref, k_ref, v_ref, g_ref, w_ref, o_ref, wb_ref,
                 kmean_ref, block_bias_ref, *, kb, n_blocks, group):
    h = pl.program_id(1)
    i = pl.program_id(2)
    slope = slopes_ref[h]
    per_step = q_ref.shape[0] // kb

    @pl.when(i == 0)
    def _():
        kmean_ref[...] = jnp.zeros_like(kmean_ref)
        for n in range(n_blocks):
            rows = k_ref[n * kb:(n + 1) * kb, :].astype(F32)
            kmean_ref[n:n + 1, :] = jnp.mean(rows, axis=0, keepdims=True)
        km_hi, km_lo = _split_bf16(kmean_ref[...])
        for n in range(per_step):
            rows = slice(n * kb, (n + 1) * kb)
            block_bias_ref[rows, :] = _moba_block_bias(q_all_ref[rows, :], km_hi, km_lo, n, slope, kb)

    def next_step_block_bias():
        km_hi, km_lo = _split_bf16(kmean_ref[...])
        for t in range(per_step):
            own = jnp.minimum((i + 1) * per_step + t, n_blocks - 1)
            rows = pl.ds(pl.multiple_of(own * kb, kb), kb)
            block_bias_ref[rows, :] = _moba_block_bias(q_all_ref[rows, :], km_hi, km_lo, own, slope, kb)

    rc =lax.broadcasted_iota(jnp.int32, (kb, kb), 0) - lax.broadcasted_iota(jnp.int32, (kb, kb), 1)
    in_block_alibi = slope * rc.astype(F32)
    own_group = (i * per_step) // group

    def attend(t, grp):
        own = i * per_step + t
        q = q_ref[t * kb:(t + 1) * kb, :]
        per_block = block_bias_ref[pl.ds(pl.multiple_of(own * kb, kb), kb), :]
        rows = slice(grp * group * kb, (grp + 1) * group * kb)
        scores = []
        for w, s in enumerate(_lane_blocks(_dot_nt(q, k_ref[rows, :]), kb)):
            n = grp * group + w
            bias = jnp.broadcast_to(per_block[:, n:n + 1], (kb, kb)) - in_block_alibi
            scores.append(s + jnp.where((n == own) & (rc < 0), NEG_INF, bias))
        return _partial_softmax(scores, v_ref[rows, :])

    for last in range(n_blocks // group):
        @pl.when(own_group == last)
        def _():
            _side_cast(w_ref, wb_ref)
            for t in range(per_step):
                out = _merged_softmax([attend(t, grp) for grp in range(last + 1)])
                o_ref[t * kb:(t + 1) * kb, :] = _rms(out, g_ref[...]).astype(o_ref.dtype)
            next_step_block_bias()


def _moba_attention(qkv, out_gain, slopes, w2d, B, S, *, kb=MOBA_BLOCK, group=4, blocks_per_step=2):
    n_blocks = S // kb
    assert S % kb == 0 and n_blocks <= LANES and n_blocks % group == 0 and group % blocks_per_step == 0
    head0 = HEADS_DILATED
    n_steps = n_blocks // blocks_per_step
    grid = (B, HEADS_MOBA, n_steps)
    q_spec, k_spec, v_spec, g_spec, o_spec = _attn_specs(S, n_steps, head0, kb * blocks_per_step)
    q_all_spec = pl.BlockSpec((S, HEAD_DIM), lambda b, h, i: (b, head0 + h))
    w_spec, wb_spec, wb_shape = _side_cast_specs(w2d, grid)
    return pl.pallas_call(
        functools.partial(_moba_kernel, kb=kb, n_blocks=n_blocks, group=group),
        grid=grid,
        in_specs=[pl.BlockSpec(memory_space=pltpu.SMEM), q_spec, q_all_spec, k_spec, v_spec, g_spec, w_spec],
        out_specs=[o_spec, wb_spec],
        scratch_shapes=[pltpu.VMEM((LANES, HEAD_DIM), F32), pltpu.VMEM((S, LANES), F32)],
        out_shape=[jax.ShapeDtypeStruct((B * S, HEADS_MOBA * HEAD_DIM), BF16), wb_shape],
        compiler_params=_params(("arbitrary", "arbitrary", "arbitrary")),
        name="moba_attn",
    )(slopes * LOG2E, qkv, qkv, qkv, qkv, out_gain, w2d)


def _stick_kernel(q_ref, k_ref, v_ref, g_ref, w_ref, o_ref, wb_ref, *, kb, group):
    _side_cast(w_ref, wb_ref)
    i = pl.program_id(2)
    q = q_ref[...]
    tq = q.shape[0]
    rc = lax.broadcasted_iota(jnp.int32, (tq, kb), 0) - lax.broadcasted_iota(jnp.int32, (tq, kb), 1)
    later = (lax.broadcasted_iota(jnp.int32, (kb, kb), 0)
             > lax.broadcasted_iota(jnp.int32, (kb, kb), 1)).astype(BF16)

    def attend(grp, carry, diagonal):
        z_blocks = _lane_blocks(_dot_nt(q, _key_rows(k_ref, grp * group, group, kb)), kb)
        weights = [None] * group
        for w in reversed(range(group)):
            z = z_blocks[w]
            log_1m = jnp.minimum(-z, 0.0) - jnp.log2(1.0 + jnp.exp2(-jnp.abs(z)))
            if diagonal:
                past = rc > w * kb
                log_1m = jnp.where(past, log_1m, 0.0)
            after = _dot(log_1m.astype(BF16), later) + carry
            a = jnp.exp2(z + log_1m + after)
            if diagonal:
                a = jnp.where(past, a, 0.0)
            weights[w] = a.astype(BF16)
            carry = carry + jnp.sum(log_1m, axis=-1, keepdims=True)
        return carry, _dot(jnp.concatenate(weights, axis=1), _key_rows(v_ref, grp * group, group, kb))

    carry, acc = attend(i, jnp.zeros((tq, 1), F32), True)

    odd = i % 2

    def single(state):
        carry, out = attend(i - 1, state[0], False)
        return carry, state[1] + out

    carry, acc = lax.cond(odd == 1, single, lambda state: state, (carry, acc))

    def body(t, state):
        first = i - 1 - odd - 2 * t
        carry, out_a = attend(first, state[0], False)
        carry, out_b = attend(first - 1, carry, False)
        return carry, state[1] + out_a + out_b

    _, acc = lax.fori_loop(0, i // 2, body, (carry, acc))
    o_ref[...] = _rms(acc, g_ref[...]).astype(o_ref.dtype)


def _stick_attention(qkv, out_gain, w2d, B, S, *, kb=ATTN_TILE, group=2):
    tq = kb * group
    nq = S // tq
    assert S % tq == 0
    head0 = HEADS_DILATED + HEADS_MOBA
    grid = (B, HEADS_STICK, nq)
    q_spec, k_spec, v_spec, g_spec, o_spec = _attn_specs(S, nq, head0, tq)
    w_spec, wb_spec, wb_shape = _side_cast_specs(w2d, grid)
    return pl.pallas_call(
        functools.partial(_stick_kernel, kb=kb, group=group),
        grid=grid,
        in_specs=[q_spec, k_spec, v_spec, g_spec, w_spec],
        out_specs=[o_spec, wb_spec],
        out_shape=[jax.ShapeDtypeStruct((B * S, HEADS_STICK * HEAD_DIM), BF16), wb_shape],
        compiler_params=_params(("arbitrary", "arbitrary", "arbitrary")),
        name="stick_attn",
    )(qkv, qkv, qkv, out_gain, w2d)


def _oproj_kernel(oa_ref, ob_ref, oc_ref, w_ref, h_ref, g_ref, hn_ref, f_ref):
    heads = jnp.concatenate([oa_ref[...], ob_ref[...], oc_ref[...]], axis=1)
    hn = h_ref[...] + _dot(heads, w_ref[...])
    hn_ref[...] = hn
    f_ref[...] = _rms(hn, g_ref[...]).astype(f_ref.dtype)


def _out_proj(oa, ob, oc, w_bf16, h, gain, f_dtype, *, tm=512):
    T, D = h.shape
    row = lambda i: (i, 0)
    fixed = lambda i: (0, 0)
    return pl.pallas_call(
        _oproj_kernel,
        grid=(T // tm,),
        in_specs=[
            pl.BlockSpec((tm, oa.shape[1]), row),
            pl.BlockSpec((tm, ob.shape[1]), row),
            pl.BlockSpec((tm, oc.shape[1]), row),
            pl.BlockSpec((D, D), fixed),
            pl.BlockSpec((tm, D), row),
            pl.BlockSpec((1, D), fixed),
        ],
        out_specs=[pl.BlockSpec((tm, D), row), pl.BlockSpec((tm, D), row)],
        out_shape=[jax.ShapeDtypeStruct((T, D), F32), jax.ShapeDtypeStruct((T, D), f_dtype)],
        compiler_params=_params(("parallel",)),
        name="out_proj",
    )(oa, ob, oc, w_bf16, h, gain.reshape(1, D))


def _swiglu_block(f, wg_ref, wu_ref, wd_ref):
    g = _dot(f, wg_ref[...])
    u = _dot(f, wu_ref[...])
    a = (g * jax.nn.sigmoid(g) * u).astype(BF16)
    return _dot(a, wd_ref[...])


def _ffn_kernel(f_ref, wg_ref, wu_ref, wd_ref, h_ref, o_ref, acc_ref):
    j = pl.program_id(1)

    @pl.when(j == 0)
    def _():
        acc_ref[...] = h_ref[...]

    acc_ref[...] += _swiglu_block(f_ref[...], wg_ref, wu_ref, wd_ref)

    @pl.when(j == pl.num_programs(1) - 1)
    def _():
        o_ref[...] = acc_ref[...]


def _ffn_dense(f, h, wg, wu, wd, *, tm=512, tf=512):
    T, D = h.shape
    F = wg.shape[1]
    return pl.pallas_call(
        _ffn_kernel,
        grid=(T // tm, F // tf),
        in_specs=[
            pl.BlockSpec((tm, D), lambda i, j: (i, 0)),
            pl.BlockSpec((D, tf), lambda i, j: (0, j)),
            pl.BlockSpec((D, tf), lambda i, j: (0, j)),
            pl.BlockSpec((tf, D), lambda i, j: (j, 0)),
            pl.BlockSpec((tm, D), lambda i, j: (i, 0)),
        ],
        out_specs=pl.BlockSpec((tm, D), lambda i, j: (i, 0)),
        out_shape=jax.ShapeDtypeStruct((T, D), F32),
        scratch_shapes=[pltpu.VMEM((tm, D), F32)],
        compiler_params=_params(("parallel", "arbitrary")),
        name="ffn_dense",
    )(f, wg, wu, wd, h)


META_EXPERT, META_RANK, META_GATE = 0, 2, 4


def _router_kernel(f_ref, w_ref, meta_ref, count_ref):
    @pl.when(pl.program_id(0) == 0)
    def _():
        count_ref[...] = jnp.zeros_like(count_ref)

    logits = _dot(f_ref[...].astype(BF16), w_ref[...])
    tm = logits.shape[0]
    lane = lax.broadcasted_iota(jnp.int32, logits.shape, 1).astype(F32)
    g = jnp.where(lane < N_EXPERTS, logits, NEG_INF)
    picks, tops, experts = [], [], []
    for _ in range(2):
        top = jnp.max(g, axis=-1, keepdims=True)
        first = jnp.min(jnp.where(g == top, lane, float(LANES)), axis=-1, keepdims=True)
        pick = lane == first
        g = jnp.where(pick, NEG_INF, g)
        picks.append(pick)
        tops.append(top)
        experts.append(first)
    e2 = jnp.exp(tops[1] - tops[0])
    denom = 1.0 + e2
    gates = [1.0 / denom, e2 / denom]

    chosen = (picks[0] | picks[1]).astype(BF16)
    r = lax.broadcasted_iota(jnp.int32, (tm, tm), 0)
    c = lax.broadcasted_iota(jnp.int32, (tm, tm), 1)
    before = _dot((r > c).astype(BF16), chosen) + count_ref[...]
    ranks = [jnp.sum(jnp.where(p, before, 0.0), axis=-1, keepdims=True) for p in picks]
    count_ref[...] += jnp.sum(chosen.astype(F32), axis=0, keepdims=True)

    meta = jnp.zeros(logits.shape, F32)
    for base, pair in ((META_EXPERT, experts), (META_RANK, ranks), (META_GATE, gates)):
        for s in range(2):
            meta = jnp.where(lane == float(base + s), pair[s], meta)
    meta_ref[...] = meta


def _router(f, router_w, *, tm=512):
    T, D = f.shape
    w = jnp.zeros((D, LANES), BF16).at[:, :N_EXPERTS].set(router_w.astype(BF16))
    return pl.pallas_call(
        _router_kernel,
        grid=(T // tm,),
        in_specs=[pl.BlockSpec((tm, D), lambda i: (i, 0)), pl.BlockSpec((D, LANES), lambda i: (0, 0))],
        out_specs=[pl.BlockSpec((tm, LANES), lambda i: (i, 0)), pl.BlockSpec((1, LANES), lambda i: (0, 0))],
        out_shape=[jax.ShapeDtypeStruct((T, LANES), F32), jax.ShapeDtypeStruct((1, LANES), F32)],
        compiler_params=_params(("arbitrary",)),
        name="moe_router",
    )(f, w)


def _dispatch_kernel(pad_ref, pos_ref, f_ref, xs_ref, zero_ref, sem):
    tt = f_ref.shape[0]

    @pl.when(pl.program_id(0) == 0)
    def _():
        zero_ref[...] = jnp.zeros_like(zero_ref)
        n_experts = pad_ref.shape[0] - 1
        fills = [pltpu.make_async_copy(
            zero_ref, xs_ref.at[pl.ds(pl.multiple_of(pad_ref[e], SUBLANES), zero_ref.shape[0])], sem)
            for e in range(n_experts)]
        tmg = zero_ref.shape[0] - SUBLANES

        def unused_tile_fill(g):
            return pltpu.make_async_copy(zero_ref.at[pl.ds(0, tmg)],
                                         xs_ref.at[pl.ds(pl.multiple_of(g * tmg, SUBLANES), tmg)], sem)

        def start_fill(g, carry):
            unused_tile_fill(g).start()
            return carry

        def wait_fill(g, carry):
            unused_tile_fill(g).wait()
            return carry

        for fill in fills:
            fill.start()
            fill.wait()
        lax.fori_loop(pad_ref[n_experts], xs_ref.shape[0] // tmg, start_fill, 0)
        lax.fori_loop(pad_ref[n_experts], xs_ref.shape[0] // tmg, wait_fill, 0)

    def issue(r, carry):
        src = f_ref.at[pl.ds(r, 1)]
        for s in range(2):
            pltpu.make_async_copy(src, xs_ref.at[pl.ds(pos_ref[0, 0, 2 * r + s], 1)], sem).start(priority=s)
        return carry

    lax.fori_loop(0, tt, issue, 0, unroll=DMA_ISSUE_UNROLL)
    for _ in range(2):
        pltpu.make_async_copy(f_ref, xs_ref.at[pl.ds(0, tt)], sem).wait()


def _dispatch(pad_start, pos3, f, n_rows, tmg):
    T, D = f.shape
    n_tiles, _, n_slots = pos3.shape
    return pl.pallas_call(
        _dispatch_kernel,
        grid=(n_tiles,),
        in_specs=[
            pl.BlockSpec(memory_space=pltpu.SMEM),
            pl.BlockSpec((1, 1, n_slots), lambda i: (i, 0, 0), memory_space=pltpu.SMEM),
            pl.BlockSpec((n_slots // 2, D), lambda i: (i, 0)),
        ],
        out_specs=pl.BlockSpec(memory_space=pl.ANY),
        out_shape=jax.ShapeDtypeStruct((n_rows, D), f.dtype),
        scratch_shapes=[pltpu.VMEM((tmg + SUBLANES, D), f.dtype), pltpu.SemaphoreType.DMA(())],
        compiler_params=_params(("arbitrary",)),
        name="moe_dispatch",
    )(pad_start, pos3, f)


def _moe_group_kernel(te_ref, nu_ref, x_ref, wg_ref, wu_ref, wd_ref, y_ref, xb_ref):
    del te_ref
    g = pl.program_id(0)
    j = pl.program_id(1)
    used = g < nu_ref[0]

    @pl.when(j == 0)
    def _():
        y_ref[...] = jnp.zeros_like(y_ref)
        xb_ref[...] = x_ref[...].astype(BF16)

    @pl.when(used)
    def _():
        y_ref[...] += _swiglu_block(xb_ref[...], wg_ref.at[0], wu_ref.at[0], wd_ref.at[0])


def _moe_grouped_ffn(tile_expert, n_used, xs, wg, wu, wd, *, tmg, tf):
    P, D = xs.shape
    E, _, F = wg.shape
    J = F // tf

    def f_block(g, j, nu):
        return jnp.where(g < nu[0], j, J - 1)

    grid_spec = pltpu.PrefetchScalarGridSpec(
        num_scalar_prefetch=2,
        grid=(P // tmg, J),
        in_specs=[
            pl.BlockSpec((tmg, D), lambda g, j, te, nu: (jnp.minimum(g, nu[0] - 1), 0)),
            pl.BlockSpec((1, D, tf), lambda g, j, te, nu: (te[g], 0, f_block(g, j, nu))),
            pl.BlockSpec((1, D, tf), lambda g, j, te, nu: (te[g], 0, f_block(g, j, nu))),
            pl.BlockSpec((1, tf, D), lambda g, j, te, nu: (te[g], f_block(g, j, nu), 0)),
        ],
        out_specs=pl.BlockSpec((tmg, D), lambda g, j, te, nu: (g, 0)),
        scratch_shapes=[pltpu.VMEM((tmg, D), BF16)],
    )
    return pl.pallas_call(
        _moe_group_kernel,
        grid_spec=grid_spec,
        out_shape=jax.ShapeDtypeStruct((P, D), F32),
        compiler_params=_params(("arbitrary", "arbitrary"), vmem=MOE_VMEM_LIMIT),
        name="moe_ffn",
    )(tile_expert, n_used, xs, wg, wu, wd)


def _combine_kernel(pos_ref, y_ref, h_ref, meta_ref, o_ref, buf_ref, sem):
    tt = h_ref.shape[0]

    def issue(r, carry):
        for s in range(2):
            pltpu.make_async_copy(y_ref.at[pl.ds(pos_ref[0, 0, 2 * r + s], 1)],
                                  buf_ref.at[s, pl.ds(r, 1)], sem).start(priority=s)
        return carry

    lax.fori_loop(0, tt, issue, 0, unroll=DMA_ISSUE_UNROLL)
    for s in range(2):
        pltpu.make_async_copy(y_ref.at[pl.ds(0, tt)], buf_ref.at[s], sem).wait()
    meta = meta_ref[...]
    o_ref[...] = (h_ref[...] + meta[:, META_GATE:META_GATE + 1] * buf_ref[0]
                  + meta[:, META_GATE + 1:META_GATE + 2] * buf_ref[1])


def _combine(pos3, y, h, meta):
    T, D = h.shape
    n_tiles, _, n_slots = pos3.shape
    tt = n_slots // 2
    return pl.pallas_call(
        _combine_kernel,
        grid=(n_tiles,),
        in_specs=[
            pl.BlockSpec((1, 1, n_slots), lambda i: (i, 0, 0), memory_space=pltpu.SMEM),
            pl.BlockSpec(memory_space=pl.ANY),
            pl.BlockSpec((tt, D), lambda i: (i, 0)),
            pl.BlockSpec((tt, LANES), lambda i: (i, 0)),
        ],
        out_specs=pl.BlockSpec((tt, D), lambda i: (i, 0)),
        out_shape=jax.ShapeDtypeStruct((T, D), F32),
        scratch_shapes=[pltpu.VMEM((2, tt, D), F32), pltpu.SemaphoreType.DMA(())],
        compiler_params=_params(("arbitrary",)),
        name="moe_combine",
    )(pos3, y, h, meta)


def _moe_routed(f, h, router_w, wg, wu, wd, *, tmg=720, tf=512, tt=256):
    T, D = h.shape
    E = wg.shape[0]
    n_tiles = 2 * T // tmg + E + 2
    meta, counts = _router(f, router_w)
    counts = counts[0, :E].astype(jnp.int32)
    tiles_per_expert = (counts + tmg - 1) // tmg
    tile_end = jnp.cumsum(tiles_per_expert)
    row_start = (tile_end - tiles_per_expert) * tmg
    n_used = tile_end[-1:]
    tile_id = jnp.minimum(jnp.arange(n_tiles, dtype=jnp.int32), n_used - 1)
    tile_expert = jnp.sum(tile_id[:, None] >= tile_end[None, :], axis=1).astype(jnp.int32)
    expert = meta[:, META_EXPERT:META_EXPERT + 2].astype(jnp.int32)
    rank = meta[:, META_RANK:META_RANK + 2].astype(jnp.int32)
    pos3 = (row_start[expert] + rank).reshape(T // tt, 1, 2 * tt)
    pad_start = jnp.concatenate([(row_start + counts) // SUBLANES * SUBLANES, n_used]).astype(jnp.int32)
    xs = _dispatch(pad_start, pos3, f, n_tiles * tmg, tmg)
    y = _moe_grouped_ffn(tile_expert, n_used.astype(jnp.int32), xs, wg, wu, wd, tmg=tmg, tf=tf)
    return _combine(pos3, y, h, meta)


def _alibi_slopes():
    n = HEADS_DILATED + HEADS_MOBA
    s = jnp.asarray(2.0 ** (-8.0 * np.arange(1, n + 1) / n), dtype=F32)
    return s[:HEADS_DILATED], s[HEADS_DILATED:]


def _head_norm_rows(q_gain, k_gain, D):
    a = HEADS_DILATED * HEAD_DIM
    b = HEADS_MOBA * HEAD_DIM
    rest = D - a - b

    def row(g):
        return jnp.concatenate([jnp.tile(g[0], HEADS_DILATED), jnp.tile(g[1], HEADS_MOBA), jnp.ones((rest,), F32)])

    flag = jnp.concatenate([jnp.ones((a + b,), F32), jnp.zeros((rest,), F32)])
    gain = jnp.concatenate([row(q_gain) * Q_PRESCALE, row(k_gain), jnp.ones((D,), F32)])
    flags = jnp.concatenate([flag, flag, jnp.zeros((D,), F32)])
    return gain.reshape(1, 3 * D), flags.reshape(1, 3 * D)


def kernel(x, attn_norm, w_in, q_gain, k_gain, out_gain, w_out, ffn_norm, dense_w_gate, dense_w_up, dense_w_down, moe_router, moe_w_gate, moe_w_up, moe_w_down):
    B, S, D = x.shape
    depth = w_in.shape[0]
    slopes_a, slopes_b = _alibi_slopes()
    h = x.reshape(B * S, D)
    for layer in range(depth):
        head_gain, head_flag = _head_norm_rows(q_gain[layer], k_gain[layer], D)
        qkv = _qkv_proj(h, attn_norm[layer], w_in[layer].astype(BF16), head_gain, head_flag)
        og = out_gain[layer].reshape(1, D)
        dense = layer % 2 == 0
        i = layer // 2
        wg, wu, wd = ((dense_w_gate[i], dense_w_up[i], dense_w_down[i]) if dense
                      else (moe_w_gate[i], moe_w_up[i], moe_w_down[i]))
        oa, wg_b = _dilated_attention(qkv, og, slopes_a, wg.reshape(-1, wg.shape[-1]), B, S)
        ob, wu_b = _moba_attention(qkv, og, slopes_b, wu.reshape(-1, wu.shape[-1]), B, S)
        oc, wd_b = _stick_attention(qkv, og, wd.reshape(-1, wd.shape[-1]), B, S)
        wg_b, wu_b, wd_b = wg_b.reshape(wg.shape), wu_b.reshape(wu.shape), wd_b.reshape(wd.shape)
        h, f = _out_proj(oa, ob, oc, w_out[layer].astype(BF16), h, ffn_norm[layer], BF16 if dense else F32)
        if dense:
            h = _ffn_dense(f, h, wg_b, wu_b, wd_b)
        else:
            h = _moe_routed(f, h, moe_router[i], wg_b, wu_b, wd_b)
    return h.reshape(B, S, D)
```

```python
import functools

import numpy as np
import jax
import jax.numpy as jnp
from jax import lax
from jax.experimental import pallas as pl
from jax.experimental.pallas import tpu as pltpu

HEAD_DIM = 128
N_HEADS = 16
HEADS_DILATED = 6
HEADS_MOBA = 5
HEADS_STICK = 5
DILATED_PATTERNS = ((128, 1), (512, 4), (2048, 16))
MOBA_BLOCK = 256
MOBA_TOPK = 3
N_EXPERTS = 8
EPS = 1e-6

LANES = 128
SUBLANES = 8
VMEM_LIMIT = 56 * 1024 * 1024
MOE_VMEM_LIMIT = 60 * 1024 * 1024
DMA_ISSUE_UNROLL = 8
ATTN_TILE = 256
NEG_INF = float("-inf")
LOG2E = 1.4426950408889634
Q_PRESCALE = HEAD_DIM ** -0.5 * LOG2E

F32 = jnp.float32
BF16 = jnp.bfloat16


def _params(sem, vmem=VMEM_LIMIT):
    return pltpu.CompilerParams(dimension_semantics=sem, vmem_limit_bytes=vmem)


def _dot(a, b):
    return jnp.dot(a, b, preferred_element_type=F32)


def _dot_nt(a, b):
    return lax.dot_general(a, b, (((1,), (1,)), ((), ())), preferred_element_type=F32)


def _rms(x, gain):
    ms = jnp.mean(x * x, axis=-1, keepdims=True)
    return x * lax.rsqrt(ms + EPS) * gain


def _split_bf16(x):
    hi = x.astype(BF16)
    lo = (x - hi.astype(F32)).astype(BF16)
    return hi, lo


def _qkv_kernel(x_ref, g_ref, w_ref, hg_ref, hf_ref, o_ref, a_ref):
    @pl.when(pl.program_id(1) == 0)
    def _():
        a_ref[...] = _rms(x_ref[...], g_ref[...]).astype(BF16)

    y = _dot(a_ref[...], w_ref[0])
    for c in range(y.shape[1] // HEAD_DIM):
        cols = slice(c * HEAD_DIM, (c + 1) * HEAD_DIM)
        blk = y[:, cols]
        ms = jnp.mean(blk * blk, axis=-1, keepdims=True)
        normed = blk * lax.rsqrt(ms + EPS)
        o_ref[:, cols] = (jnp.where(hf_ref[:, cols] > 0, normed, blk) * hg_ref[:, cols]).astype(o_ref.dtype)


def _qkv_proj(h, gain, w_bf16, layer, head_gain, head_flag, *, tm=1024, tn=1536):
    T, D = h.shape
    N = w_bf16.shape[2]
    return pl.pallas_call(
        _qkv_kernel,
        grid=(T // tm, N // tn),
        in_specs=[
            pl.BlockSpec((tm, D), lambda i, j: (i, 0)),
            pl.BlockSpec((1, D), lambda i, j: (0, 0)),
            pl.BlockSpec((1, D, tn), lambda i, j: (layer, 0, j)),
            pl.BlockSpec((1, tn), lambda i, j: (0, j)),
            pl.BlockSpec((1, tn), lambda i, j: (0, j)),
        ],
        out_specs=pl.BlockSpec((tm, tn), lambda i, j: (i, j)),
        out_shape=jax.ShapeDtypeStruct((T, N), BF16),
        scratch_shapes=[pltpu.VMEM((tm, D), BF16)],
        compiler_params=_params(("parallel", "arbitrary")),
        name="qkv_proj",
    )(h, gain.reshape(1, D), w_bf16, head_gain, head_flag)


def _attn_specs(S, n_q_tiles, head0, tq):
    q_spec = pl.BlockSpec((tq, HEAD_DIM), lambda b, h, i, *_: (b * n_q_tiles + i, head0 + h))
    k_spec = pl.BlockSpec((S, HEAD_DIM), lambda b, h, i, *_: (b, N_HEADS + head0 + h))
    v_spec = pl.BlockSpec((S, HEAD_DIM), lambda b, h, i, *_: (b, 2 * N_HEADS + head0 + h))
    g_spec = pl.BlockSpec((1, HEAD_DIM), lambda b, h, i, *_: (0, head0 + h))
    o_spec = pl.BlockSpec((tq, HEAD_DIM), lambda b, h, i, *_: (b * n_q_tiles + i, h))
    return q_spec, k_spec, v_spec, g_spec, o_spec


def _side_cast_specs(w2d, grid):
    rows, cols = w2d.shape
    steps = int(np.prod(grid))
    n_blocks = max(n for n in range(1, steps + 1) if rows % n == 0 and (rows // n) % 16 == 0)

    def index(*ids):
        step = ids[0]
        for size, idx in zip(grid[1:], ids[1:]):
            step = step * size + idx
        return jnp.minimum(step, n_blocks - 1), 0

    spec = pl.BlockSpec((rows // n_blocks, cols), index)
    return spec, spec, jax.ShapeDtypeStruct(w2d.shape, BF16)


def _side_cast(w_ref, wb_ref):
    wb_ref[...] = w_ref[...].astype(BF16)


def _key_rows(ref, first_block, n_blocks, tq):
    return ref[pl.ds(pl.multiple_of(first_block * tq, tq), n_blocks * tq), :]


def _lane_blocks(x, tq):
    return [x[:, w * tq:(w + 1) * tq] for w in range(x.shape[1] // tq)]


def _partial_softmax(scores, v_rows):
    m = jnp.max(functools.reduce(jnp.maximum, scores), axis=-1, keepdims=True)
    m_safe = jnp.where(m == NEG_INF, 0.0, m)
    p = [jnp.exp2(s - m_safe) for s in scores]
    l = jnp.sum(functools.reduce(jnp.add, p), axis=-1, keepdims=True)
    return m, l, _dot(jnp.concatenate([x.astype(BF16) for x in p], axis=1), v_rows)


def _merged_softmax(parts):
    m = functools.reduce(jnp.maximum, [part[0] for part in parts])
    weights = [jnp.exp2(part[0] - m) for part in parts]
    l = functools.reduce(jnp.add, [w * part[1] for w, part in zip(weights, parts)])
    acc = functools.reduce(jnp.add, [w * part[2] for w, part in zip(weights, parts)])
    return acc / l


def _dilated_kernel(q_ref, k_ref, v_ref, g_ref, bias_ref, w_ref, o_ref, wb_ref, *, tq, n_win):
    _side_cast(w_ref, wb_ref)
    per_step = q_ref.shape[0] // tq
    for t in range(per_step):
        rows = slice(t * tq, (t + 1) * tq)
        tile = pl.program_id(2) * per_step + t
        before = jnp.minimum(tile, n_win - 1)
        s_all = _dot_nt(q_ref[rows, :], _key_rows(k_ref, tile - before, n_win, tq))
        scores = [s + bias_ref[0, before + n_win - 1 - w] for w, s in enumerate(_lane_blocks(s_all, tq))]
        part = _partial_softmax(scores, _key_rows(v_ref, tile - before, n_win, tq))
        o_ref[rows, :] = _rms(part[2] / part[1], g_ref[...]).astype(o_ref.dtype)


def _dilated_bias(slopes, tq, n_back, n_win):
    d = jnp.arange(n_back + n_win, dtype=jnp.int32)[:, None, None] - (n_win - 1)
    r = jnp.arange(tq, dtype=jnp.int32)[None, :, None]
    c = jnp.arange(tq, dtype=jnp.int32)[None, None, :]
    dist = d * tq + r - c
    mult = jnp.zeros(dist.shape, F32)
    for window, dilation in DILATED_PATTERNS:
        hit = (dist >= 0) & (dist <= window) & (dist % dilation == 0)
        mult = mult + hit.astype(F32)
    logm = jnp.where(mult > 0, jnp.log2(jnp.maximum(mult, 1.0)), NEG_INF)
    return logm[None] - (slopes * LOG2E)[:, None, None, None] * dist.astype(F32)[None]


def _dilated_attention(qkv, out_gain, slopes, w2d, B, S, *, tq=ATTN_TILE, tiles_per_step=2):
    nq = S // tq
    assert S % (tq * tiles_per_step) == 0
    max_window = max(w for w, _ in DILATED_PATTERNS)
    n_back = -(-max_window // tq)
    n_win = min(n_back + 1, nq)
    bias = _dilated_bias(slopes, tq, n_back, n_win)
    n_steps = nq // tiles_per_step
    grid = (B, HEADS_DILATED, n_steps)
    q_spec, k_spec, v_spec, g_spec, o_spec = _attn_specs(S, n_steps, 0, tq * tiles_per_step)
    bias_spec = pl.BlockSpec((1, n_back + n_win, tq, tq), lambda b, h, i: (h, 0, 0, 0))
    w_spec, wb_spec, wb_shape = _side_cast_specs(w2d, grid)
    return pl.pallas_call(
        functools.partial(_dilated_kernel, tq=tq, n_win=n_win),
        grid=grid,
        in_specs=[q_spec, k_spec, v_spec, g_spec, bias_spec, w_spec],
        out_specs=[o_spec, wb_spec],
        out_shape=[jax.ShapeDtypeStruct((B * S, HEADS_DILATED * HEAD_DIM), BF16), wb_shape],
        compiler_params=_params(("arbitrary", "arbitrary", "arbitrary")),
        name="dilated_attn",
    )(qkv, qkv, qkv, out_gain, bias, w2d)


def _moba_block_bias(q, km_hi, km_lo, own, slope, kb):
    gate = _dot_nt(q, km_hi) + _dot_nt(q, km_lo)
    lane = lax.broadcasted_iota(jnp.int32, gate.shape, 1)
    lane_f = lane.astype(F32)
    g = jnp.where(lane < own, gate, NEG_INF)
    sel = jnp.zeros(gate.shape, jnp.bool_)
    for _ in range(MOBA_TOPK):
        top = jnp.max(g, axis=-1, keepdims=True)
        is_top = (g == top) & (top > NEG_INF)
        first = jnp.min(jnp.where(is_top, lane_f, float(LANES)), axis=-1, keepdims=True)
        pick = lane_f == first
        sel = sel | pick
        g = jnp.where(pick, NEG_INF, g)
    block_alibi = slope * ((own - lane) * kb).astype(F32)
    return jnp.where(lane < own, jnp.where(sel, 0.0, NEG_INF) - block_alibi,
                     jnp.where(lane == own, 0.0, NEG_INF))


def _moba_kernel(slopes_ref, q_ref, q_all_ref, k_ref, v_ref, g_ref, w_ref, o_ref, wb_ref,
                 kmean_ref, block_bias_ref, *, kb, n_blocks, group):
    h = pl.program_id(1)
    i = pl.program_id(2)
    slope = slopes_ref[h]
    per_step = q_ref.shape[0] // kb

    @pl.when(i == 0)
    def _():
        kmean_ref[...] = jnp.zeros_like(kmean_ref)
        for n in range(n_blocks):
            rows = k_ref[n * kb:(n + 1) * kb, :].astype(F32)
            kmean_ref[n:n + 1, :] = jnp.mean(rows, axis=0, keepdims=True)
        km_hi, km_lo = _split_bf16(kmean_ref[...])
        for n in range(per_step):
            rows = slice(n * kb, (n + 1) * kb)
            block_bias_ref[rows, :] = _moba_block_bias(q_all_ref[rows, :], km_hi, km_lo, n, slope, kb)

    def next_step_block_bias():
        km_hi, km_lo = _split_bf16(kmean_ref[...])
        for t in range(per_step):
            own = jnp.minimum((i + 1) * per_step + t, n_blocks - 1)
            rows = pl.ds(pl.multiple_of(own * kb, kb), kb)
            block_bias_ref[rows, :] = _moba_block_bias(q_all_ref[rows, :], km_hi, km_lo, own, slope, kb)

    rc =lax.broadcasted_iota(jnp.int32, (kb, kb), 0) - lax.broadcasted_iota(jnp.int32, (kb, kb), 1)
    in_block_alibi = slope * rc.astype(F32)
    own_group = (i * per_step) // group

    def attend(t, grp):
        own = i * per_step + t
        q = q_ref[t * kb:(t + 1) * kb, :]
        per_block = block_bias_ref[pl.ds(pl.multiple_of(own * kb, kb), kb), :]
        rows = slice(grp * group * kb, (grp + 1) * group * kb)
        scores = []
        for w, s in enumerate(_lane_blocks(_dot_nt(q, k_ref[rows, :]), kb)):
            n = grp * group + w
            bias = jnp.broadcast_to(per_block[:, n:n + 1], (kb, kb)) - in_block_alibi
            scores.append(s + jnp.where((n == own) & (rc < 0), NEG_INF, bias))
        return _partial_softmax(scores, v_ref[rows, :])

    for last in range(n_blocks // group):
        @pl.when(own_group == last)
        def _():
            _side_cast(w_ref, wb_ref)
            for t in range(per_step):
                out = _merged_softmax([attend(t, grp) for grp in range(last + 1)])
                o_ref[t * kb:(t + 1) * kb, :] = _rms(out, g_ref[...]).astype(o_ref.dtype)
            next_step_block_bias()


def _moba_attention(qkv, out_gain, slopes, w2d, B, S, *, kb=MOBA_BLOCK, group=4, blocks_per_step=2):
    n_blocks = S // kb
    assert S % kb == 0 and n_blocks <= LANES and n_blocks % group == 0 and group % blocks_per_step == 0
    head0 = HEADS_DILATED
    n_steps = n_blocks // blocks_per_step
    grid = (B, HEADS_MOBA, n_steps)
    q_spec, k_spec, v_spec, g_spec, o_spec = _attn_specs(S, n_steps, head0, kb * blocks_per_step)
    q_all_spec = pl.BlockSpec((S, HEAD_DIM), lambda b, h, i: (b, head0 + h))
    w_spec, wb_spec, wb_shape = _side_cast_specs(w2d, grid)
    return pl.pallas_call(
        functools.partial(_moba_kernel, kb=kb, n_blocks=n_blocks, group=group),
        grid=grid,
        in_specs=[pl.BlockSpec(memory_space=pltpu.SMEM), q_spec, q_all_spec, k_spec, v_spec, g_spec, w_spec],
        out_specs=[o_spec, wb_spec],
        scratch_shapes=[pltpu.VMEM((LANES, HEAD_DIM), F32), pltpu.VMEM((S, LANES), F32)],
        out_shape=[jax.ShapeDtypeStruct((B * S, HEADS_MOBA * HEAD_DIM), BF16), wb_shape],
        compiler_params=_params(("arbitrary", "arbitrary", "arbitrary")),
        name="moba_attn",
    )(slopes * LOG2E, qkv, qkv, qkv, qkv, out_gain, w2d)


def _stick_kernel(q_ref, k_ref, v_ref, g_ref, w_ref, o_ref, wb_ref, *, kb, group):
    _side_cast(w_ref, wb_ref)
    i = pl.program_id(2)
    q = q_ref[...]
    tq = q.shape[0]
    rc = lax.broadcasted_iota(jnp.int32, (tq, kb), 0) - lax.broadcasted_iota(jnp.int32, (tq, kb), 1)
    later = (lax.broadcasted_iota(jnp.int32, (kb, kb), 0)
             > lax.broadcasted_iota(jnp.int32, (kb, kb), 1)).astype(BF16)

    def attend(grp, carry, diagonal):
        z_blocks = _lane_blocks(_dot_nt(q, _key_rows(k_ref, grp * group, group, kb)), kb)
        weights = [None] * group
        for w in reversed(range(group)):
            z = z_blocks[w]
            log_1m = jnp.minimum(-z, 0.0) - jnp.log2(1.0 + jnp.exp2(-jnp.abs(z)))
            if diagonal:
                past = rc > w * kb
                log_1m = jnp.where(past, log_1m, 0.0)
            after = _dot(log_1m.astype(BF16), later) + carry
            a = jnp.exp2(z + log_1m + after)
            if diagonal:
                a = jnp.where(past, a, 0.0)
            weights[w] = a.astype(BF16)
            carry = carry + jnp.sum(log_1m, axis=-1, keepdims=True)
        return carry, _dot(jnp.concatenate(weights, axis=1), _key_rows(v_ref, grp * group, group, kb))

    carry, acc = attend(i, jnp.zeros((tq, 1), F32), True)

    odd = i % 2

    def single(state):
        carry, out = attend(i - 1, state[0], False)
        return carry, state[1] + out

    carry, acc = lax.cond(odd == 1, single, lambda state: state, (carry, acc))

    def body(t, state):
        first = i - 1 - odd - 2 * t
        carry, out_a = attend(first, state[0], False)
        carry, out_b = attend(first - 1, carry, False)
        return carry, state[1] + out_a + out_b

    _, acc = lax.fori_loop(0, i // 2, body, (carry, acc))
    o_ref[...] = _rms(acc, g_ref[...]).astype(o_ref.dtype)


def _stick_attention(qkv, out_gain, w2d, B, S, *, kb=ATTN_TILE, group=2):
    tq = kb * group
    nq = S // tq
    assert S % tq == 0
    head0 = HEADS_DILATED + HEADS_MOBA
    grid = (B, HEADS_STICK, nq)
    q_spec, k_spec, v_spec, g_spec, o_spec = _attn_specs(S, nq, head0, tq)
    w_spec, wb_spec, wb_shape = _side_cast_specs(w2d, grid)
    return pl.pallas_call(
        functools.partial(_stick_kernel, kb=kb, group=group),
        grid=grid,
        in_specs=[q_spec, k_spec, v_spec, g_spec, w_spec],
        out_specs=[o_spec, wb_spec],
        out_shape=[jax.ShapeDtypeStruct((B * S, HEADS_STICK * HEAD_DIM), BF16), wb_shape],
        compiler_params=_params(("arbitrary", "arbitrary", "arbitrary")),
        name="stick_attn",
    )(qkv, qkv, qkv, out_gain, w2d)


def _oproj_kernel(oa_ref, ob_ref, oc_ref, w_ref, h_ref, g_ref, hn_ref, f_ref):
    heads = jnp.concatenate([oa_ref[...], ob_ref[...], oc_ref[...]], axis=1)
    hn = h_ref[...] + _dot(heads, w_ref[0])
    hn_ref[...] = hn
    f_ref[...] = _rms(hn, g_ref[...]).astype(f_ref.dtype)


def _out_proj(oa, ob, oc, w_bf16, layer, h, gain, f_dtype, *, tm=512):
    T, D = h.shape
    row = lambda i: (i, 0)
    fixed = lambda i: (0, 0)
    return pl.pallas_call(
        _oproj_kernel,
        grid=(T // tm,),
        in_specs=[
            pl.BlockSpec((tm, oa.shape[1]), row),
            pl.BlockSpec((tm, ob.shape[1]), row),
            pl.BlockSpec((tm, oc.shape[1]), row),
            pl.BlockSpec((1, D, D), lambda i: (layer, 0, 0)),
            pl.BlockSpec((tm, D), row),
            pl.BlockSpec((1, D), fixed),
        ],
        out_specs=[pl.BlockSpec((tm, D), row), pl.BlockSpec((tm, D), row)],
        out_shape=[jax.ShapeDtypeStruct((T, D), F32), jax.ShapeDtypeStruct((T, D), f_dtype)],
        compiler_params=_params(("parallel",)),
        name="out_proj",
    )(oa, ob, oc, w_bf16, h, gain.reshape(1, D))


def _swiglu_block(f, wg_ref, wu_ref, wd_ref):
    g = _dot(f, wg_ref[...])
    u = _dot(f, wu_ref[...])
    a = (g * jax.nn.sigmoid(g) * u).astype(BF16)
    return _dot(a, wd_ref[...])


def _ffn_kernel(f_ref, wg_ref, wu_ref, wd_ref, h_ref, o_ref, acc_ref):
    j = pl.program_id(1)

    @pl.when(j == 0)
    def _():
        acc_ref[...] = h_ref[...]

    acc_ref[...] += _swiglu_block(f_ref[...], wg_ref, wu_ref, wd_ref)

    @pl.when(j == pl.num_programs(1) - 1)
    def _():
        o_ref[...] = acc_ref[...]


def _ffn_dense(f, h, wg, wu, wd, *, tm=512, tf=512):
    T, D = h.shape
    F = wg.shape[1]
    return pl.pallas_call(
        _ffn_kernel,
        grid=(T // tm, F // tf),
        in_specs=[
            pl.BlockSpec((tm, D), lambda i, j: (i, 0)),
            pl.BlockSpec((D, tf), lambda i, j: (0, j)),
            pl.BlockSpec((D, tf), lambda i, j: (0, j)),
            pl.BlockSpec((tf, D), lambda i, j: (j, 0)),
            pl.BlockSpec((tm, D), lambda i, j: (i, 0)),
        ],
        out_specs=pl.BlockSpec((tm, D), lambda i, j: (i, 0)),
        out_shape=jax.ShapeDtypeStruct((T, D), F32),
        scratch_shapes=[pltpu.VMEM((tm, D), F32)],
        compiler_params=_params(("parallel", "arbitrary")),
        name="ffn_dense",
    )(f, wg, wu, wd, h)


META_EXPERT, META_RANK, META_GATE = 0, 2, 4


def _router_kernel(f_ref, w_ref, meta_ref, count_ref):
    @pl.when(pl.program_id(0) == 0)
    def _():
        count_ref[...] = jnp.zeros_like(count_ref)

    logits = _dot(f_ref[...].astype(BF16), w_ref[...])
    tm = logits.shape[0]
    lane = lax.broadcasted_iota(jnp.int32, logits.shape, 1).astype(F32)
    g = jnp.where(lane < N_EXPERTS, logits, NEG_INF)
    picks, tops, experts = [], [], []
    for _ in range(2):
        top = jnp.max(g, axis=-1, keepdims=True)
        first = jnp.min(jnp.where(g == top, lane, float(LANES)), axis=-1, keepdims=True)
        pick = lane == first
        g = jnp.where(pick, NEG_INF, g)
        picks.append(pick)
        tops.append(top)
        experts.append(first)
    e2 = jnp.exp(tops[1] - tops[0])
    denom = 1.0 + e2
    gates = [1.0 / denom, e2 / denom]

    chosen = (picks[0] | picks[1]).astype(BF16)
    r = lax.broadcasted_iota(jnp.int32, (tm, tm), 0)
    c = lax.broadcasted_iota(jnp.int32, (tm, tm), 1)
    before = _dot((r > c).astype(BF16), chosen) + count_ref[...]
    ranks = [jnp.sum(jnp.where(p, before, 0.0), axis=-1, keepdims=True) for p in picks]
    count_ref[...] += jnp.sum(chosen.astype(F32), axis=0, keepdims=True)

    meta = jnp.zeros(logits.shape, F32)
    for base, pair in ((META_EXPERT, experts), (META_RANK, ranks), (META_GATE, gates)):
        for s in range(2):
            meta = jnp.where(lane == float(base + s), pair[s], meta)
    meta_ref[...] = meta


def _router(f, router_w, *, tm=512):
    T, D = f.shape
    w = jnp.zeros((D, LANES), BF16).at[:, :N_EXPERTS].set(router_w.astype(BF16))
    return pl.pallas_call(
        _router_kernel,
        grid=(T // tm,),
        in_specs=[pl.BlockSpec((tm, D), lambda i: (i, 0)), pl.BlockSpec((D, LANES), lambda i: (0, 0))],
        out_specs=[pl.BlockSpec((tm, LANES), lambda i: (i, 0)), pl.BlockSpec((1, LANES), lambda i: (0, 0))],
        out_shape=[jax.ShapeDtypeStruct((T, LANES), F32), jax.ShapeDtypeStruct((1, LANES), F32)],
        compiler_params=_params(("arbitrary",)),
        name="moe_router",
    )(f, w)


def _dispatch_kernel(pad_ref, pos_ref, f_ref, xs_ref, zero_ref, sem):
    tt = f_ref.shape[0]

    @pl.when(pl.program_id(0) == 0)
    def _():
        zero_ref[...] = jnp.zeros_like(zero_ref)
        n_experts = pad_ref.shape[0] - 1
        fills = [pltpu.make_async_copy(
            zero_ref, xs_ref.at[pl.ds(pl.multiple_of(pad_ref[e], SUBLANES), zero_ref.shape[0])], sem)
            for e in range(n_experts)]
        tmg = zero_ref.shape[0] - SUBLANES

        def unused_tile_fill(g):
            return pltpu.make_async_copy(zero_ref.at[pl.ds(0, tmg)],
                                         xs_ref.at[pl.ds(pl.multiple_of(g * tmg, SUBLANES), tmg)], sem)

        def start_fill(g, carry):
            unused_tile_fill(g).start()
            return carry

        def wait_fill(g, carry):
            unused_tile_fill(g).wait()
            return carry

        for fill in fills:
            fill.start()
            fill.wait()
        lax.fori_loop(pad_ref[n_experts], xs_ref.shape[0] // tmg, start_fill, 0)
        lax.fori_loop(pad_ref[n_experts], xs_ref.shape[0] // tmg, wait_fill, 0)

    def issue(r, carry):
        src = f_ref.at[pl.ds(r, 1)]
        for s in range(2):
            pltpu.make_async_copy(src, xs_ref.at[pl.ds(pos_ref[0, 0, 2 * r + s], 1)], sem).start(priority=s)
        return carry

    lax.fori_loop(0, tt, issue, 0, unroll=DMA_ISSUE_UNROLL)
    for _ in range(2):
        pltpu.make_async_copy(f_ref, xs_ref.at[pl.ds(0, tt)], sem).wait()


def _dispatch(pad_start, pos3, f, n_rows, tmg):
    T, D = f.shape
    n_tiles, _, n_slots = pos3.shape
    return pl.pallas_call(
        _dispatch_kernel,
        grid=(n_tiles,),
        in_specs=[
            pl.BlockSpec(memory_space=pltpu.SMEM),
            pl.BlockSpec((1, 1, n_slots), lambda i: (i, 0, 0), memory_space=pltpu.SMEM),
            pl.BlockSpec((n_slots // 2, D), lambda i: (i, 0)),
        ],
        out_specs=pl.BlockSpec(memory_space=pl.ANY),
        out_shape=jax.ShapeDtypeStruct((n_rows, D), f.dtype),
        scratch_shapes=[pltpu.VMEM((tmg + SUBLANES, D), f.dtype), pltpu.SemaphoreType.DMA(())],
        compiler_params=_params(("arbitrary",)),
        name="moe_dispatch",
    )(pad_start, pos3, f)


def _moe_group_kernel(te_ref, nu_ref, x_ref, wg_ref, wu_ref, wd_ref, y_ref, xb_ref):
    del te_ref
    g = pl.program_id(0)
    j = pl.program_id(1)
    used = g < nu_ref[0]

    @pl.when(j == 0)
    def _():
        y_ref[...] = jnp.zeros_like(y_ref)
        xb_ref[...] = x_ref[...].astype(BF16)

    @pl.when(used)
    def _():
        y_ref[...] += _swiglu_block(xb_ref[...], wg_ref.at[0], wu_ref.at[0], wd_ref.at[0])


def _moe_grouped_ffn(tile_expert, n_used, xs, wg, wu, wd, *, tmg, tf):
    P, D = xs.shape
    E, _, F = wg.shape
    J = F // tf

    def f_block(g, j, nu):
        return jnp.where(g < nu[0], j, J - 1)

    grid_spec = pltpu.PrefetchScalarGridSpec(
        num_scalar_prefetch=2,
        grid=(P // tmg, J),
        in_specs=[
            pl.BlockSpec((tmg, D), lambda g, j, te, nu: (jnp.minimum(g, nu[0] - 1), 0)),
            pl.BlockSpec((1, D, tf), lambda g, j, te, nu: (te[g], 0, f_block(g, j, nu))),
            pl.BlockSpec((1, D, tf), lambda g, j, te, nu: (te[g], 0, f_block(g, j, nu))),
            pl.BlockSpec((1, tf, D), lambda g, j, te, nu: (te[g], f_block(g, j, nu), 0)),
        ],
        out_specs=pl.BlockSpec((tmg, D), lambda g, j, te, nu: (g, 0)),
        scratch_shapes=[pltpu.VMEM((tmg, D), BF16)],
    )
    return pl.pallas_call(
        _moe_group_kernel,
        grid_spec=grid_spec,
        out_shape=jax.ShapeDtypeStruct((P, D), F32),
        compiler_params=_params(("arbitrary", "arbitrary"), vmem=MOE_VMEM_LIMIT),
        name="moe_ffn",
    )(tile_expert, n_used, xs, wg, wu, wd)


def _combine_kernel(pos_ref, y_ref, h_ref, meta_ref, o_ref, buf_ref, sem):
    tt = h_ref.shape[0]

    def issue(r, carry):
        for s in range(2):
            pltpu.make_async_copy(y_ref.at[pl.ds(pos_ref[0, 0, 2 * r + s], 1)],
                                  buf_ref.at[s, pl.ds(r, 1)], sem).start(priority=s)
        return carry

    lax.fori_loop(0, tt, issue, 0, unroll=DMA_ISSUE_UNROLL)
    for s in range(2):
        pltpu.make_async_copy(y_ref.at[pl.ds(0, tt)], buf_ref.at[s], sem).wait()
    meta = meta_ref[...]
    o_ref[...] = (h_ref[...] + meta[:, META_GATE:META_GATE + 1] * buf_ref[0]
                  + meta[:, META_GATE + 1:META_GATE + 2] * buf_ref[1])


def _combine(pos3, y, h, meta):
    T, D = h.shape
    n_tiles, _, n_slots = pos3.shape
    tt = n_slots // 2
    return pl.pallas_call(
        _combine_kernel,
        grid=(n_tiles,),
        in_specs=[
            pl.BlockSpec((1, 1, n_slots), lambda i: (i, 0, 0), memory_space=pltpu.SMEM),
            pl.BlockSpec(memory_space=pl.ANY),
            pl.BlockSpec((tt, D), lambda i: (i, 0)),
            pl.BlockSpec((tt, LANES), lambda i: (i, 0)),
        ],
        out_specs=pl.BlockSpec((tt, D), lambda i: (i, 0)),
        out_shape=jax.ShapeDtypeStruct((T, D), F32),
        scratch_shapes=[pltpu.VMEM((2, tt, D), F32), pltpu.SemaphoreType.DMA(())],
        compiler_params=_params(("arbitrary",)),
        name="moe_combine",
    )(pos3, y, h, meta)


def _moe_routed(f, h, router_w, wg, wu, wd, *, tmg=720, tf=512, tt=256):
    T, D = h.shape
    E = wg.shape[0]
    n_tiles = 2 * T // tmg + E + 2
    meta, counts = _router(f, router_w)
    counts = counts[0, :E].astype(jnp.int32)
    tiles_per_expert = (counts + tmg - 1) // tmg
    tile_end = jnp.cumsum(tiles_per_expert)
    row_start = (tile_end - tiles_per_expert) * tmg
    n_used = tile_end[-1:]
    tile_id = jnp.minimum(jnp.arange(n_tiles, dtype=jnp.int32), n_used - 1)
    tile_expert = jnp.sum(tile_id[:, None] >= tile_end[None, :], axis=1).astype(jnp.int32)
    expert = meta[:, META_EXPERT:META_EXPERT + 2].astype(jnp.int32)
    rank = meta[:, META_RANK:META_RANK + 2].astype(jnp.int32)
    pos3 = (row_start[expert] + rank).reshape(T // tt, 1, 2 * tt)
    pad_start = jnp.concatenate([(row_start + counts) // SUBLANES * SUBLANES, n_used]).astype(jnp.int32)
    xs = _dispatch(pad_start, pos3, f, n_tiles * tmg, tmg)
    y = _moe_grouped_ffn(tile_expert, n_used.astype(jnp.int32), xs, wg, wu, wd, tmg=tmg, tf=tf)
    return _combine(pos3, y, h, meta)


def _alibi_slopes():
    n = HEADS_DILATED + HEADS_MOBA
    s = jnp.asarray(2.0 ** (-8.0 * np.arange(1, n + 1) / n), dtype=F32)
    return s[:HEADS_DILATED], s[HEADS_DILATED:]


def _head_norm_rows(q_gain, k_gain, D):
    a = HEADS_DILATED * HEAD_DIM
    b = HEADS_MOBA * HEAD_DIM
    rest = D - a - b

    def row(g):
        return jnp.concatenate([jnp.tile(g[0], HEADS_DILATED), jnp.tile(g[1], HEADS_MOBA), jnp.ones((rest,), F32)])

    flag = jnp.concatenate([jnp.ones((a + b,), F32), jnp.zeros((rest,), F32)])
    gain = jnp.concatenate([row(q_gain) * Q_PRESCALE, row(k_gain), jnp.ones((D,), F32)])
    flags = jnp.concatenate([flag, flag, jnp.zeros((D,), F32)])
    return gain.reshape(1, 3 * D), flags.reshape(1, 3 * D)


def kernel(x, attn_norm, w_in, q_gain, k_gain, out_gain, w_out, ffn_norm, dense_w_gate, dense_w_up, dense_w_down, moe_router, moe_w_gate, moe_w_up, moe_w_down):
    B, S, D = x.shape
    depth = w_in.shape[0]
    slopes_a, slopes_b = _alibi_slopes()
    h = x.reshape(B * S, D)
    w_in_b, w_out_b = w_in.astype(BF16), w_out.astype(BF16)
    for layer in range(depth):
        head_gain, head_flag = _head_norm_rows(q_gain[layer], k_gain[layer], D)
        qkv = _qkv_proj(h, attn_norm[layer], w_in_b, layer, head_gain, head_flag)
        og = out_gain[layer].reshape(1, D)
        dense = layer % 2 == 0
        i = layer // 2
        wg, wu, wd = ((dense_w_gate[i], dense_w_up[i], dense_w_down[i]) if dense
                      else (moe_w_gate[i], moe_w_up[i], moe_w_down[i]))
        oa, wg_b = _dilated_attention(qkv, og, slopes_a, wg.reshape(-1, wg.shape[-1]), B, S)
        ob, wu_b = _moba_attention(qkv, og, slopes_b, wu.reshape(-1, wu.shape[-1]), B, S)
        oc, wd_b = _stick_attention(qkv, og, wd.reshape(-1, wd.shape[-1]), B, S)
        wg_b, wu_b, wd_b = wg_b.reshape(wg.shape), wu_b.reshape(wu.shape), wd_b.reshape(wd.shape)
        h, f = _out_proj(oa, ob, oc, w_out_b, layer, h, ffn_norm[layer], BF16 if dense else F32)
        if dense:
            h = _ffn_dense(f, h, wg_b, wu_b, wd_b)
        else:
            h = _moe_routed(f, h, moe_router[i], wg_b, wu_b, wd_b)
    return h.reshape(B, S, D)
```
